```python
import jax, jax.numpy as jnp
from jax import lax
import numpy as np

D_MODEL = 2048
BATCH = 4
SEQ = 4096
DEPTH = 1

EPS = 1e-6
ROPE_THETA = 500000.0
GLA_HEADS = 4
GLA_DK = 128
GLA_DV = 256
GLA_LOWRANK = 16
GLA_TAU = 16.0
GLA_CHUNK = 64
NSA_HEADS = 16
NSA_GROUPS = 4
NSA_HPG = NSA_HEADS // NSA_GROUPS
NSA_DH = 128
ROPE_DIM = NSA_DH // 4
CMP_BLOCK = 32
CMP_STRIDE = 16
SLC_BLOCK = 64
SLC_TOPN = 16
WINDOW = 512
NSA_QCHUNK = 64
PEER_HEADS = 8
PEER_NKEYS = 128
PEER_EXPERTS = PEER_NKEYS * PEER_NKEYS
PEER_DKEY = 128
PEER_TOPK = 16
PEER_CHUNK = 128
NSA_Q_COLS = NSA_HEADS * NSA_DH
NSA_KV_COLS = NSA_GROUPS * NSA_DH
NSA_GATE_COLS = NSA_HEADS * 3
GLA_QK_COLS = GLA_HEADS * GLA_DK
GLA_V_COLS = GLA_HEADS * GLA_DV
MERGE_COLS = D_MODEL
IN_COLS = NSA_Q_COLS + 6 * NSA_KV_COLS + NSA_GATE_COLS + 2 * GLA_QK_COLS + 2 * GLA_V_COLS + GLA_LOWRANK + 2 * MERGE_COLS

kernel_name = "hybrid_gla_nsa_peer_block"


def rmsnorm(x, g):
    xf = x.astype(jnp.float32)
    y = xf * lax.rsqrt(jnp.mean(xf * xf, axis=-1, keepdims=True) + EPS)
    return (y * g.astype(jnp.float32)).astype(x.dtype)


def partial_rope(x, positions):
    half = ROPE_DIM // 2
    inv = jnp.power(ROPE_THETA, -(jnp.arange(0, ROPE_DIM, 2, dtype=jnp.float32) / ROPE_DIM))
    ang = positions.astype(jnp.float32)[..., None] * inv
    cos = jnp.cos(ang)[:, :, None, :]
    sin = jnp.sin(ang)[:, :, None, :]
    xr = x[..., :ROPE_DIM].astype(jnp.float32)
    x1, x2 = xr[..., :half], xr[..., half:]
    rot = jnp.concatenate([x1 * cos - x2 * sin, x2 * cos + x1 * sin], axis=-1).astype(x.dtype)
    return jnp.concatenate([rot, x[..., ROPE_DIM:]], axis=-1)


def masked_softmax(s, mask):
    s = jnp.where(mask, s, -jnp.inf)
    m = jnp.max(s, axis=-1, keepdims=True)
    m = jnp.where(jnp.isfinite(m), m, 0.0)
    p = jnp.where(mask, jnp.exp(s - m), 0.0)
    return p / jnp.maximum(jnp.sum(p, axis=-1, keepdims=True), 1e-30)


def gla_mixer(q, k, v, log_a, r, norm_g):
    B, S, H, _ = q.shape
    C = GLA_CHUNK
    nc = S // C

    def to_chunks(t):
        return t.astype(jnp.float32).reshape(B, nc, C, H, t.shape[-1]).transpose(1, 0, 3, 2, 4)

    qc = to_chunks(q * (GLA_DK ** -0.5))
    kc, vc, gc = to_chunks(k), to_chunks(v), to_chunks(log_a)
    causal = jnp.tril(jnp.ones((C, C), dtype=bool))

    def step(state, inp):
        qi, ki, vi, gi = inp
        b = jnp.cumsum(gi, axis=2)
        diff = b[:, :, :, None, :] - b[:, :, None, :, :]
        decay = jnp.exp(jnp.where(causal[:, :, None], diff, -jnp.inf))
        attn = jnp.einsum('bhtd,bhsd,bhtsd->bhts', qi, ki, decay)
        o = jnp.einsum('bhts,bhse->bhte', attn, vi) + jnp.einsum('bhtd,bhde->bhte', qi * jnp.exp(b), state)
        b_last = b[:, :, -1:, :]
        new_state = jnp.exp(b_last[:, :, 0, :])[..., None] * state + jnp.einsum('bhsd,bhse->bhde', ki * jnp.exp(b_last - b), vi)
        return new_state, o

    state0 = jnp.zeros((B, H, GLA_DK, GLA_DV), jnp.float32)
    _, o = lax.scan(step, state0, (qc, kc, vc, gc))
    o = o.transpose(1, 0, 3, 2, 4).reshape(B, S, H, GLA_DV)
    o = rmsnorm(o, norm_g).reshape(B, S, H * GLA_DV)
    return (o * jax.nn.silu(r.astype(jnp.float32))).astype(r.dtype)


def nsa_mixer(q, k_cmp, v_cmp, k_slc, v_slc, k_win, v_win, gates, pos_k, pos_v, k_w1, k_w2, v_w1, v_w2):
    B, S, H, dh = q.shape
    G = NSA_GROUPS
    dtype = q.dtype
    scale = dh ** -0.5
    qg = q.reshape(B, S, G, NSA_HPG, dh).transpose(0, 2, 3, 1, 4)

    def grp(t):
        return t.transpose(0, 2, 1, 3)

    n_cmp = (S - CMP_BLOCK) // CMP_STRIDE + 1
    blk_idx = jnp.arange(n_cmp)[:, None] * CMP_STRIDE + jnp.arange(CMP_BLOCK)[None, :]

    def compress(t, pos, w1, w2):
        blocks = t[:, :, blk_idx] + pos
        hid = jax.nn.gelu(blocks.reshape(B, G, n_cmp, CMP_BLOCK * dh) @ w1)
        return hid @ w2

    kc = compress(grp(k_cmp), pos_k, k_w1, k_w2)
    vc = compress(grp(v_cmp), pos_v, v_w1, v_w2)
    t_pos = jnp.arange(S)
    cmp_end = jnp.arange(n_cmp) * CMP_STRIDE + CMP_BLOCK - 1
    cmask = cmp_end[None, :] <= t_pos[:, None]
    s_cmp = jnp.einsum('bghsd,bgnd->bghsn', qg, kc).astype(jnp.float32) * scale
    p_cmp = masked_softmax(s_cmp, cmask)
    o_cmp = jnp.einsum('bghsn,bgnd->bghsd', p_cmp.astype(dtype), vc)

    n_slc = S // SLC_BLOCK
    top_n = min(SLC_TOPN, n_slc)
    ratio = SLC_BLOCK // CMP_STRIDE
    nrep = CMP_BLOCK // CMP_STRIDE
    p_grp = jnp.sum(p_cmp, axis=2)
    padded = jnp.pad(p_grp, ((0, 0), (0, 0), (0, 0), (nrep - 1, nrep - 1)))
    imp = jnp.zeros(p_grp.shape[:-1] + (n_slc,), jnp.float32)
    for m in range(ratio):
        for n in range(nrep):
            start = m - n + nrep - 1
            imp = imp + padded[..., start::ratio][..., :n_slc]
    blk_t = t_pos // SLC_BLOCK
    j = jnp.arange(n_slc)
    forced = (j[None, :] == 0) | (j[None, :] == blk_t[:, None]) | (j[None, :] == blk_t[:, None] - 1)
    future = j[None, :] > blk_t[:, None]
    imp = jnp.where(forced, 1e9, jnp.where(future, -1e9, imp))
    _, sel = lax.top_k(imp, top_n)

    k_blocks = grp(k_slc).reshape(B, G, n_slc, SLC_BLOCK, dh)
    v_blocks = grp(v_slc).reshape(B, G, n_slc, SLC_BLOCK, dh)
    kw_pad = jnp.pad(grp(k_win), ((0, 0), (0, 0), (WINDOW, 0), (0, 0)))
    vw_pad = jnp.pad(grp(v_win), ((0, 0), (0, 0), (WINDOW, 0), (0, 0)))
    QC = NSA_QCHUNK
    nq = S // QC
    q_chunks = qg.reshape(B, G, NSA_HPG, nq, QC, dh).transpose(3, 0, 1, 2, 4, 5)
    sel_chunks = sel.reshape(B, G, nq, QC, top_n).transpose(2, 0, 1, 3, 4)
    starts = jnp.arange(nq, dtype=jnp.int32) * QC
    bi = jnp.arange(B)[:, None, None, None]
    gi = jnp.arange(G)[None, :, None, None]

    def sweep(args):
        qc, sc, c0 = args
        tq = c0 + jnp.arange(QC)
        ks = k_blocks[bi, gi, sc].reshape(B, G, QC, top_n * SLC_BLOCK, dh)
        vs = v_blocks[bi, gi, sc].reshape(B, G, QC, top_n * SLC_BLOCK, dh)
        kpos = (sc[..., None] * SLC_BLOCK + jnp.arange(SLC_BLOCK)).reshape(B, G, QC, top_n * SLC_BLOCK)
        smask = (kpos <= tq[None, None, :, None])[:, :, None]
        s = jnp.einsum('bghqd,bgqkd->bghqk', qc, ks).astype(jnp.float32) * scale
        o_s = jnp.einsum('bghqk,bgqkd->bghqd', masked_softmax(s, smask).astype(dtype), vs)
        kw = lax.dynamic_slice_in_dim(kw_pad, c0, QC + WINDOW, axis=2)
        vw = lax.dynamic_slice_in_dim(vw_pad, c0, QC + WINDOW, axis=2)
        ppos = c0 + jnp.arange(QC + WINDOW)
        wmask = (ppos[None, :] > tq[:, None]) & (ppos[None, :] <= tq[:, None] + WINDOW) & (ppos[None, :] >= WINDOW)
        s = jnp.einsum('bghqd,bgkd->bghqk', qc, kw).astype(jnp.float32) * scale
        o_w = jnp.einsum('bghqk,bgkd->bghqd', masked_softmax(s, wmask).astype(dtype), vw)
        return o_s, o_w

    o_slc, o_win = lax.map(sweep, (q_chunks, sel_chunks, starts))
    o_slc = o_slc.transpose(1, 2, 3, 0, 4, 5).reshape(B, G, NSA_HPG, S, dh)
    o_win = o_win.transpose(1, 2, 3, 0, 4, 5).reshape(B, G, NSA_HPG, S, dh)

    g = jax.nn.sigmoid(gates.astype(jnp.float32)).reshape(B, S, G, NSA_HPG, 3).transpose(0, 2, 3, 1, 4)
    o = g[..., 0:1] * o_cmp + g[..., 1:2] * o_slc + g[..., 2:3] * o_win
    return o.transpose(0, 3, 1, 2, 4).reshape(B, S, H * dh).astype(dtype)


def peer(xn, w_q, sub_k1, sub_k2, u_tab, v_tab):
    B, S, D = xn.shape
    T = B * S
    xt = xn.reshape(T, D)
    qp = (xt @ w_q).reshape(T, PEER_HEADS, PEER_DKEY)
    half = PEER_DKEY // 2
    s1 = jnp.einsum('thd,hnd->thn', qp[..., :half], sub_k1).astype(jnp.float32)
    s2 = jnp.einsum('thd,hnd->thn', qp[..., half:], sub_k2).astype(jnp.float32)
    v1, i1 = lax.top_k(s1, PEER_TOPK)
    v2, i2 = lax.top_k(s2, PEER_TOPK)
    cand = (v1[..., :, None] + v2[..., None, :]).reshape(T, PEER_HEADS, PEER_TOPK * PEER_TOPK)
    vals, flat = lax.top_k(cand, PEER_TOPK)
    e1 = jnp.take_along_axis(i1, flat // PEER_TOPK, axis=-1)
    e2 = jnp.take_along_axis(i2, flat % PEER_TOPK, axis=-1)
    experts = e1 * PEER_NKEYS + e2
    gates = jax.nn.softmax(vals, axis=-1)
    nchunk = T // PEER_CHUNK

    def expert_chunk(args):
        xc, ec, gc = args
        hid = jax.nn.gelu(jnp.einsum('cd,chkd->chk', xc, u_tab[ec]).astype(jnp.float32))
        return jnp.einsum('chk,chkd->cd', (gc * hid).astype(xc.dtype), v_tab[ec])

    out = lax.map(expert_chunk, (xt.reshape(nchunk, PEER_CHUNK, D),
                                 experts.reshape(nchunk, PEER_CHUNK, PEER_HEADS, PEER_TOPK),
                                 gates.reshape(nchunk, PEER_CHUNK, PEER_HEADS, PEER_TOPK)))
    return out.reshape(B, S, D)


def setup_inputs(seed: int = 0) -> dict:
    key = jax.random.key(seed)
    ks = jax.random.split(key, 24)
    f32 = jnp.float32
    L, D = DEPTH, D_MODEL

    def nrm(k, shape, scale):
        return jax.random.normal(k, shape, f32) * scale

    offs = jax.random.randint(ks[1], (BATCH, 1), 0, 1024, dtype=jnp.int32)
    positions = offs + jnp.arange(SEQ, dtype=jnp.int32)[None, :]
    return {
        "x": nrm(ks[0], (BATCH, SEQ, D), 1.0),
        "positions": positions,
        "norm1_g": 1.0 + nrm(ks[2], (L, D), 0.02),
        "w_in": nrm(ks[3], (L, D, IN_COLS), D ** -0.5),
        "gla_w_decay": nrm(ks[4], (L, GLA_LOWRANK, GLA_QK_COLS), GLA_LOWRANK ** -0.5),
        "gla_b_decay": nrm(ks[5], (L, GLA_QK_COLS), 0.1),
        "gla_norm_g": 1.0 + nrm(ks[6], (L, GLA_DV), 0.02),
        "nsa_cmp_pos_k": nrm(ks[7], (L, CMP_BLOCK, NSA_DH), 0.02),
        "nsa_cmp_pos_v": nrm(ks[8], (L, CMP_BLOCK, NSA_DH), 0.02),
        "nsa_cmp_k_w1": nrm(ks[9], (L, CMP_BLOCK * NSA_DH, NSA_DH), (CMP_BLOCK * NSA_DH) ** -0.5),
        "nsa_cmp_k_w2": nrm(ks[10], (L, NSA_DH, NSA_DH), NSA_DH ** -0.5),
        "nsa_cmp_v_w1": nrm(ks[11], (L, CMP_BLOCK * NSA_DH, NSA_DH), (CMP_BLOCK * NSA_DH) ** -0.5),
        "nsa_cmp_v_w2": nrm(ks[12], (L, NSA_DH, NSA_DH), NSA_DH ** -0.5),
        "w_up_gla": nrm(ks[13], (L, GLA_V_COLS, D), GLA_V_COLS ** -0.5),
        "w_up_nsa": nrm(ks[14], (L, NSA_Q_COLS, D), NSA_Q_COLS ** -0.5),
        "w_out": nrm(ks[15], (L, D, D), D ** -0.5),
        "norm2_g": 1.0 + nrm(ks[16], (L, D), 0.02),
        "peer_w_q": nrm(ks[17], (L, D, PEER_HEADS * PEER_DKEY), D ** -0.5),
        "peer_sub_k1": nrm(ks[18], (L, PEER_HEADS, PEER_NKEYS, PEER_DKEY // 2), (PEER_DKEY // 2) ** -0.5),
        "peer_sub_k2": nrm(ks[19], (L, PEER_HEADS, PEER_NKEYS, PEER_DKEY // 2), (PEER_DKEY // 2) ** -0.5),
        "peer_u": nrm(ks[20], (L, PEER_EXPERTS, D), D ** -0.5),
        "peer_v": nrm(ks[21], (L, PEER_EXPERTS, D), PEER_HEADS ** -0.5),
        "final_g": 1.0 + nrm(ks[22], (D,), 0.02),
    }


def reference(x, positions, norm1_g, w_in, gla_w_decay, gla_b_decay, gla_norm_g, nsa_cmp_pos_k, nsa_cmp_pos_v,
              nsa_cmp_k_w1, nsa_cmp_k_w2, nsa_cmp_v_w1, nsa_cmp_v_w2, w_up_gla, w_up_nsa, w_out, norm2_g,
              peer_w_q, peer_sub_k1, peer_sub_k2, peer_u, peer_v, final_g):
    B, S, D = x.shape
    widths = [NSA_Q_COLS] + [NSA_KV_COLS] * 6 + [NSA_GATE_COLS, GLA_QK_COLS, GLA_QK_COLS, GLA_V_COLS,
                                                 GLA_LOWRANK, GLA_V_COLS, MERGE_COLS, MERGE_COLS]
    offsets = np.cumsum(widths)[:-1].tolist()
    for l in range(DEPTH):
        h = rmsnorm(x, norm1_g[l])
        proj = h @ w_in[l]
        (nq, kcm, vcm, ksl, vsl, kwn, vwn, ngate, gq, gk, gv, glr, gr, gate_a, gate_b) = jnp.split(proj, offsets, axis=-1)

        log_a = jax.nn.log_sigmoid((glr @ gla_w_decay[l] + gla_b_decay[l]).astype(jnp.float32)) / GLA_TAU
        y_gla = gla_mixer(gq.reshape(B, S, GLA_HEADS, GLA_DK), gk.reshape(B, S, GLA_HEADS, GLA_DK),
                          gv.reshape(B, S, GLA_HEADS, GLA_DV), log_a.reshape(B, S, GLA_HEADS, GLA_DK),
                          gr, gla_norm_g[l])

        def kv(t):
            return t.reshape(B, S, NSA_GROUPS, NSA_DH)
        q_n = partial_rope(nq.reshape(B, S, NSA_HEADS, NSA_DH), positions)
        y_nsa = nsa_mixer(q_n, partial_rope(kv(kcm), positions), kv(vcm), partial_rope(kv(ksl), positions), kv(vsl),
                          partial_rope(kv(kwn), positions), kv(vwn), ngate,
                          nsa_cmp_pos_k[l], nsa_cmp_pos_v[l], nsa_cmp_k_w1[l], nsa_cmp_k_w2[l],
                          nsa_cmp_v_w1[l], nsa_cmp_v_w2[l])

        ga = jax.nn.sigmoid(gate_a.astype(jnp.float32))
        gb = jax.nn.sigmoid(gate_b.astype(jnp.float32))
        merged = (ga * (y_gla @ w_up_gla[l]) + gb * (y_nsa @ w_up_nsa[l])).astype(x.dtype)
        x = x + merged @ w_out[l]

        h2 = rmsnorm(x, norm2_g[l])
        x = x + peer(h2, peer_w_q[l], peer_sub_k1[l], peer_sub_k2[l], peer_u[l], peer_v[l])
    return rmsnorm(x, final_g)
```

```python
import functools

import jax
import jax.numpy as jnp
import numpy as np
from jax import lax
from jax.experimental import pallas as pl
from jax.experimental.pallas import tpu as pltpu

f32 = jnp.float32
bf16 = jnp.bfloat16
i32 = jnp.int32

D_MODEL = 2048
EPS = 1e-6
ROPE_THETA = 500000.0
GLA_HEADS = 4
GLA_DK = 128
GLA_DV = 256
GLA_LOWRANK = 16
GLA_TAU = 16.0
GLA_CHUNK = 64
GLA_SUB = 8
NSA_HEADS = 16
NSA_GROUPS = 4
NSA_HPG = 4
NSA_DH = 128
ROPE_DIM = 32
CMP_BLOCK = 32
CMP_STRIDE = 16
SLC_BLOCK = 64
SLC_TOPN = 16
WINDOW = 512
PEER_HEADS = 8
PEER_NKEYS = 128
PEER_DKEY = 128
PEER_TOPK = 16

LANES = 128
OFF_NQ = 0
OFF_KCM = 2048
OFF_VCM = 2560
OFF_KSL = 3072
OFF_VSL = 3584
OFF_KWN = 4096
OFF_VWN = 4608
OFF_GQ = 5120
OFF_GK = 5632
OFF_GV = 6144
OFF_GR = 7168
OFF_GA = 8192
OFF_GB = 10240
OFF_NGATE = 12288
OFF_GLR = 12416
NCOLS = 12544

VMEM_LIMIT = 56 * 1024 * 1024
NEG = -1e30
HI = lax.Precision.HIGHEST


def _cp(sem, vmem=VMEM_LIMIT):
    return pltpu.CompilerParams(dimension_semantics=sem, vmem_limit_bytes=vmem)


def _nt(a, b, precision=None):
    return lax.dot_general(a, b, (((1,), (1,)), ((), ())), preferred_element_type=f32, precision=precision)


def _tn(a, b, precision=None):
    return lax.dot_general(a, b, (((0,), (0,)), ((), ())), preferred_element_type=f32, precision=precision)


def _dot(a, b, precision=None):
    return jnp.dot(a, b, preferred_element_type=f32, precision=precision)


def _proj_kernel(x_ref, g_ref, w_ref, o_ref, h_scr):
    @pl.when(pl.program_id(1) == 0)
    def _():
        x = x_ref[...]
        ms = jnp.mean(x * x, axis=-1, keepdims=True)
        h_scr[...] = (x * lax.rsqrt(ms + EPS) * g_ref[...]).astype(bf16)

    o_ref[...] = _dot(h_scr[...], w_ref[...])


def _proj(x2, g1, w_pack, tm, tn):
    T = x2.shape[0]
    return pl.pallas_call(
        _proj_kernel,
        out_shape=jax.ShapeDtypeStruct((T, NCOLS), f32),
        grid=(T // tm, NCOLS // tn),
        in_specs=[
            pl.BlockSpec((tm, D_MODEL), lambda i, j: (i, 0)),
            pl.BlockSpec((1, D_MODEL), lambda i, j: (0, 0)),
            pl.BlockSpec((D_MODEL, tn), lambda i, j: (0, j)),
        ],
        out_specs=pl.BlockSpec((tm, tn), lambda i, j: (i, j)),
        scratch_shapes=[pltpu.VMEM((tm, D_MODEL), bf16)],
        compiler_params=_cp(("parallel", "arbitrary")),
        name="proj",
    )(x2, g1, w_pack)


def _gla_kernel(gq_ref, gk_ref, gv_ref, gr_ref, glr_ref, wd_ref, bd_ref, ng_ref, tri_ref, o_ref, st_ref):
    C = GLA_CHUNK
    R = GLA_SUB

    @pl.when(pl.program_id(1) == 0)
    def _():
        st_ref[...] = jnp.zeros_like(st_ref)

    glr = glr_ref[...]
    tri = tri_ref[...]
    lane = lax.broadcasted_iota(i32, (R, C), 1)
    row = lax.broadcasted_iota(i32, (R, C), 0)
    for h in range(GLA_HEADS):
        ks = slice(h * GLA_DK, (h + 1) * GLA_DK)
        vs = slice(h * GLA_DV, (h + 1) * GLA_DV)
        z = _dot(glr, wd_ref[:, ks], HI) + bd_ref[:, ks]
        g = (jnp.minimum(z, 0.0) - jnp.log(1.0 + jnp.exp(-jnp.abs(z)))) / GLA_TAU
        b = _dot(tri, g, HI)
        q = gq_ref[:, ks] * (GLA_DK ** -0.5)
        k = gk_ref[:, ks]
        v = gv_ref[:, vs].astype(bf16)
        st = st_ref[h]
        o = _nt((q * jnp.exp(b)).astype(bf16), st.astype(bf16))
        blocks = []
        for i in range(C // R):
            r0 = i * R
            bi = b[r0:r0 + R]
            ref = b[r0:r0 + 1]
            qi = q[r0:r0 + R]
            qt = qi * jnp.exp(bi - ref)
            kt = k * jnp.exp(jnp.minimum(ref - b, 0.0))
            a = _nt(qt.astype(bf16), kt.astype(bf16))
            a = jnp.where(lane < r0, a, 0.0)
            for s in range(R):
                w = qi * k[r0 + s:r0 + s + 1] * jnp.exp(jnp.minimum(bi - b[r0 + s:r0 + s + 1], 0.0))
                col = jnp.sum(w, axis=-1, keepdims=True)
                a = jnp.where(lane == r0 + s, jnp.where(row >= s, col, 0.0), a)
            blocks.append(a)
        attn = jnp.concatenate(blocks, axis=0)
        o = o + _dot(attn.astype(bf16), v)
        bl = b[C - 1:C]
        kd = k * jnp.exp(bl - b)
        st_ref[h] = jnp.exp(bl) * st + _tn(v, kd.astype(bf16))
        ms = jnp.mean(o * o, axis=-1, keepdims=True)
        y = o * lax.rsqrt(ms + EPS) * ng_ref[...]
        r = gr_ref[:, vs]
        o_ref[:, vs] = y * (r * jax.nn.sigmoid(r))


def _gla(proj, wd_pad, bd, ng, B, S):
    C = GLA_CHUNK
    nc = S // C
    tri = jnp.asarray(np.tril(np.ones((C, C), np.float32)))
    qk_w = GLA_HEADS * GLA_DK
    v_w = GLA_HEADS * GLA_DV
    row = lambda b, c: b * nc + c
    return pl.pallas_call(
        _gla_kernel,
        out_shape=jax.ShapeDtypeStruct((B * S, v_w), f32),
        grid=(B, nc),
        in_specs=[
            pl.BlockSpec((C, qk_w), lambda b, c: (row(b, c), OFF_GQ // qk_w)),
            pl.BlockSpec((C, qk_w), lambda b, c: (row(b, c), OFF_GK // qk_w)),
            pl.BlockSpec((C, v_w), lambda b, c: (row(b, c), OFF_GV // v_w)),
            pl.BlockSpec((C, v_w), lambda b, c: (row(b, c), OFF_GR // v_w)),
            pl.BlockSpec((C, LANES), lambda b, c: (row(b, c), OFF_GLR // LANES)),
            pl.BlockSpec((LANES, qk_w), lambda b, c: (0, 0)),
            pl.BlockSpec((1, qk_w), lambda b, c: (0, 0)),
            pl.BlockSpec((1, GLA_DV), lambda b, c: (0, 0)),
            pl.BlockSpec((C, C), lambda b, c: (0, 0)),
        ],
        out_specs=pl.BlockSpec((C, v_w), lambda b, c: (row(b, c), 0)),
        scratch_shapes=[pltpu.VMEM((GLA_HEADS, GLA_DV, GLA_DK), f32)],
        compiler_params=_cp(("parallel", "arbitrary")),
        name="gla",
    )(proj, proj, proj, proj, proj, wd_pad, bd, ng, tri)


def _rope_kernel(q_ref, kc_ref, ks_ref, kw_ref, pos_ref, inv_ref, qo_ref, ko_ref):
    ang = pos_ref[...].astype(f32) * inv_ref[...]
    cos = jnp.cos(ang)
    sin = jnp.sin(ang)
    lane = lax.broadcasted_iota(i32, ang.shape, 1)
    half = ROPE_DIM // 2
    s_up = jnp.where((lane >= half) & (lane < ROPE_DIM), sin, 0.0)
    s_dn = jnp.where(lane < half, -sin, 0.0)

    def rot(x):
        return x * cos + pltpu.roll(x, half, 1) * s_up + pltpu.roll(x, LANES - half, 1) * s_dn

    for h in range(NSA_HEADS):
        sl = slice(h * NSA_DH, (h + 1) * NSA_DH)
        qo_ref[:, sl] = rot(q_ref[:, sl])
    kv_w = NSA_GROUPS * NSA_DH
    for n, ref in enumerate((kc_ref, ks_ref, kw_ref)):
        for g in range(NSA_GROUPS):
            sl = slice(g * NSA_DH, (g + 1) * NSA_DH)
            ko_ref[:, n * kv_w + g * NSA_DH:n * kv_w + (g + 1) * NSA_DH] = rot(ref[:, sl])


def _rope(proj, pos_col, inv_lane, tm):
    T = proj.shape[0]
    q_w = NSA_HEADS * NSA_DH
    kv_w = NSA_GROUPS * NSA_DH
    return pl.pallas_call(
        _rope_kernel,
        out_shape=(jax.ShapeDtypeStruct((T, q_w), f32), jax.ShapeDtypeStruct((T, 3 * kv_w), f32)),
        grid=(T // tm,),
        in_specs=[
            pl.BlockSpec((tm, q_w), lambda i: (i, OFF_NQ // q_w)),
            pl.BlockSpec((tm, kv_w), lambda i: (i, OFF_KCM // kv_w)),
            pl.BlockSpec((tm, kv_w), lambda i: (i, OFF_KSL // kv_w)),
            pl.BlockSpec((tm, kv_w), lambda i: (i, OFF_KWN // kv_w)),
            pl.BlockSpec((tm, 1), lambda i: (i, 0)),
            pl.BlockSpec((1, LANES), lambda i: (0, 0)),
        ],
        out_specs=(pl.BlockSpec((tm, q_w), lambda i: (i, 0)), pl.BlockSpec((tm, 3 * kv_w), lambda i: (i, 0))),
        compiler_params=_cp(("parallel",)),
        name="rope",
    )(proj, proj, proj, proj, pos_col, inv_lane)


def _compress_kernel(r_ref, pos_ref, w1_ref, w2_ref, o_ref):
    half = (CMP_BLOCK // 2) * NSA_DH
    r = r_ref[0]
    n_rows = r.shape[0]
    ra = (r + pos_ref[0:1, :]).astype(bf16)
    rb = (pltpu.roll(r, n_rows - 1, 0) + pos_ref[1:2, :]).astype(bf16)
    hid = _dot(ra, w1_ref[0:half, :]) + _dot(rb, w1_ref[half:2 * half, :])
    o_ref[0] = _dot(jax.nn.gelu(hid).astype(bf16), w2_ref[...])


def _compress(r, pos2, w1, w2):
    BG, n_rows, width = r.shape
    return pl.pallas_call(
        _compress_kernel,
        out_shape=jax.ShapeDtypeStruct((BG, n_rows, NSA_DH), f32),
        grid=(BG,),
        in_specs=[
            pl.BlockSpec((1, n_rows, width), lambda i: (i, 0, 0)),
            pl.BlockSpec((2, width), lambda i: (0, 0)),
            pl.BlockSpec((2 * width, NSA_DH), lambda i: (0, 0)),
            pl.BlockSpec((NSA_DH, NSA_DH), lambda i: (0, 0)),
        ],
        out_specs=pl.BlockSpec((1, n_rows, NSA_DH), lambda i: (i, 0, 0)),
        compiler_params=_cp(("parallel",)),
        name="compress",
    )(r, pos2, w1, w2)


def _cmp_sel_kernel(q_ref, kc_ref, vc_ref, at_ref, o_ref, sel_ref, *, tq, n_slc):
    i = pl.program_id(2)
    ncp = kc_ref.shape[1]
    scale = NSA_DH ** -0.5
    t = i * tq + lax.broadcasted_iota(i32, (tq, ncp), 0)
    n_end = lax.broadcasted_iota(i32, (tq, ncp), 1) * CMP_STRIDE + (CMP_BLOCK - 1)
    cmask = n_end <= t
    kc = kc_ref[0].astype(bf16)
    vc = vc_ref[0].astype(bf16)
    pg = jnp.zeros((tq, ncp), f32)
    for hp in range(NSA_HPG):
        sl = slice(hp * NSA_DH, (hp + 1) * NSA_DH)
        s = _nt(q_ref[:, sl].astype(bf16), kc) * scale
        s = jnp.where(cmask, s, NEG)
        m = jnp.max(s, axis=-1, keepdims=True)
        p = jnp.where(cmask, jnp.exp(s - m), 0.0)
        p = p / jnp.maximum(jnp.sum(p, axis=-1, keepdims=True), 1e-30)
        o_ref[:, sl] = _dot(p.astype(bf16), vc)
        pg = pg + p
    imp = _nt(at_ref[...], pg, HI)
    j = lax.broadcasted_iota(i32, (n_slc, tq), 0)
    blk = (i * tq + lax.broadcasted_iota(i32, (n_slc, tq), 1)) // SLC_BLOCK
    forced = (j == 0) | (j == blk) | (j == blk - 1)
    imp = jnp.where(forced, 1e9, jnp.where(j > blk, -1e9, imp))
    sel = jnp.zeros((n_slc, tq), f32)
    cur = imp
    for _ in range(min(SLC_TOPN, n_slc)):
        m = jnp.max(cur, axis=0, keepdims=True)
        first = jnp.min(jnp.where(cur == m, j, n_slc), axis=0, keepdims=True)
        pick = j == first
        sel = jnp.where(pick, 1.0, sel)
        cur = jnp.where(pick, -jnp.inf, cur)
    sel_ref[0] = sel


def _cmp_sel(q_rot, kc, vc, at, B, S, tq):
    nq = S // tq
    G = NSA_GROUPS
    ncp = kc.shape[1]
    n_slc = S // SLC_BLOCK
    gw = NSA_HPG * NSA_DH
    return pl.pallas_call(
        functools.partial(_cmp_sel_kernel, tq=tq, n_slc=n_slc),
        out_shape=(jax.ShapeDtypeStruct((B * S, NSA_HEADS * NSA_DH), f32),
                   jax.ShapeDtypeStruct((B * G, n_slc, S), f32)),
        grid=(B, G, nq),
        in_specs=[
            pl.BlockSpec((tq, gw), lambda b, g, i: (b * nq + i, g)),
            pl.BlockSpec((1, ncp, NSA_DH), lambda b, g, i: (b * G + g, 0, 0)),
            pl.BlockSpec((1, ncp, NSA_DH), lambda b, g, i: (b * G + g, 0, 0)),
            pl.BlockSpec((n_slc, ncp), lambda b, g, i: (0, 0)),
        ],
        out_specs=(pl.BlockSpec((tq, gw), lambda b, g, i: (b * nq + i, g)),
                   pl.BlockSpec((1, n_slc, tq), lambda b, g, i: (b * G + g, 0, i))),
        compiler_params=_cp(("parallel", "parallel", "parallel")),
        name="cmp_sel",
    )(q_rot, kc, vc, at)


def _slc_kernel(q_ref, k_ref, v_ref, sel_ref, o_ref, m_scr, l_scr, acc_scr, *, tq, tk, n_slc):
    i = pl.program_id(2)
    j = pl.program_id(3)
    nk = pl.num_programs(3)
    scale = NSA_DH ** -0.5

    @pl.when(j == 0)
    def _():
        m_scr[...] = jnp.full_like(m_scr, NEG)
        l_scr[...] = jnp.zeros_like(l_scr)
        acc_scr[...] = jnp.zeros_like(acc_scr)

    @pl.when(j * tk < (i + 1) * tq)
    def _():
        kblk = (j * tk + lax.broadcasted_iota(i32, (n_slc, tk), 1)) // SLC_BLOCK
        expand = jnp.where(kblk == lax.broadcasted_iota(i32, (n_slc, tk), 0), 1.0, 0.0).astype(bf16)
        picked = _dot(sel_ref[0].astype(bf16), expand)
        t = i * tq + lax.broadcasted_iota(i32, (tq, tk), 0)
        kpos = j * tk + lax.broadcasted_iota(i32, (tq, tk), 1)
        mask = jnp.where(kpos <= t, picked, 0.0) > 0.5
        k = k_ref[...].astype(bf16)
        v = v_ref[...].astype(bf16)
        for hp in range(NSA_HPG):
            sl = slice(hp * NSA_DH, (hp + 1) * NSA_DH)
            s = _nt(q_ref[:, sl].astype(bf16), k) * scale
            s = jnp.where(mask, s, NEG)
            m_old = m_scr[hp]
            m_new = jnp.maximum(m_old, jnp.max(s, axis=-1, keepdims=True))
            alpha = jnp.exp(m_old - m_new)
            p = jnp.where(mask, jnp.exp(s - m_new), 0.0)
            l_scr[hp] = alpha * l_scr[hp] + jnp.sum(p, axis=-1, keepdims=True)
            acc_scr[hp] = alpha * acc_scr[hp] + _dot(p.astype(bf16), v)
            m_scr[hp] = m_new

    @pl.when(j == nk - 1)
    def _():
        for hp in range(NSA_HPG):
            sl = slice(hp * NSA_DH, (hp + 1) * NSA_DH)
            o_ref[:, sl] = acc_scr[hp] / jnp.maximum(l_scr[hp], 1e-30)


def _slc(q_rot, k_rot, proj, sel_tm, B, S, tq, tk):
    nq, nk = S // tq, S // tk
    G = NSA_GROUPS
    n_slc = S // SLC_BLOCK
    gw = NSA_HPG * NSA_DH
    kv_w = NSA_GROUPS * NSA_DH

    def kj(i, j):
        return jnp.minimum(j, ((i + 1) * tq - 1) // tk)

    return pl.pallas_call(
        functools.partial(_slc_kernel, tq=tq, tk=tk, n_slc=n_slc),
        out_shape=jax.ShapeDtypeStruct((B * S, NSA_HEADS * NSA_DH), f32),
        grid=(B, G, nq, nk),
        in_specs=[
            pl.BlockSpec((tq, gw), lambda b, g, i, j: (b * nq + i, g)),
            pl.BlockSpec((tk, NSA_DH), lambda b, g, i, j: (b * nk + kj(i, j), kv_w // NSA_DH + g)),
            pl.BlockSpec((tk, NSA_DH), lambda b, g, i, j: (b * nk + kj(i, j), OFF_VSL // NSA_DH + g)),
            pl.BlockSpec((1, tq, n_slc), lambda b, g, i, j: (b * G + g, i, 0)),
        ],
        out_specs=pl.BlockSpec((tq, gw), lambda b, g, i, j: (b * nq + i, g)),
        scratch_shapes=[pltpu.VMEM((NSA_HPG, tq, 1), f32), pltpu.VMEM((NSA_HPG, tq, 1), f32),
                        pltpu.VMEM((NSA_HPG, tq, NSA_DH), f32)],
        compiler_params=_cp(("parallel", "parallel", "parallel", "arbitrary")),
        name="slc",
    )(q_rot, k_rot, proj, sel_tm)


def _win_kernel(q_ref, k0_ref, k1_ref, k2_ref, v0_ref, v1_ref, v2_ref, o_ref, *, tq):
    i = pl.program_id(2)
    scale = NSA_DH ** -0.5
    nb = WINDOW // tq + 1
    k = jnp.concatenate([k0_ref[...], k1_ref[...], k2_ref[...]], axis=0).astype(bf16)
    v = jnp.concatenate([v0_ref[...], v1_ref[...], v2_ref[...]], axis=0).astype(bf16)
    t = i * tq + lax.broadcasted_iota(i32, (tq, nb * tq), 0)
    kpos = (i - (nb - 1)) * tq + lax.broadcasted_iota(i32, (tq, nb * tq), 1)
    mask = jnp.where(kpos >= 0, jnp.where(kpos <= t, jnp.where(kpos > t - WINDOW, 1.0, 0.0), 0.0), 0.0) > 0.5
    for hp in range(NSA_HPG):
        sl = slice(hp * NSA_DH, (hp + 1) * NSA_DH)
        s = _nt(q_ref[:, sl].astype(bf16), k) * scale
        s = jnp.where(mask, s, NEG)
        m = jnp.max(s, axis=-1, keepdims=True)
        p = jnp.where(mask, jnp.exp(s - m), 0.0)
        p = p / jnp.maximum(jnp.sum(p, axis=-1, keepdims=True), 1e-30)
        o_ref[:, sl] = _dot(p.astype(bf16), v)


def _win(q_rot, k_rot, proj, B, S, tq):
    assert WINDOW % tq == 0 and WINDOW // tq == 2
    nq = S // tq
    gw = NSA_HPG * NSA_DH
    kv_w = NSA_GROUPS * NSA_DH

    def kspec(d, col0):
        return pl.BlockSpec((tq, NSA_DH), lambda b, g, i: (b * nq + jnp.maximum(i - 2 + d, 0), col0 + g))

    return pl.pallas_call(
        functools.partial(_win_kernel, tq=tq),
        out_shape=jax.ShapeDtypeStruct((B * S, NSA_HEADS * NSA_DH), f32),
        grid=(B, NSA_GROUPS, nq),
        in_specs=[pl.BlockSpec((tq, gw), lambda b, g, i: (b * nq + i, g))]
        + [kspec(d, 2 * kv_w // NSA_DH) for d in range(3)]
        + [kspec(d, OFF_VWN // NSA_DH) for d in range(3)],
        out_specs=pl.BlockSpec((tq, gw), lambda b, g, i: (b * nq + i, g)),
        compiler_params=_cp(("parallel", "parallel", "parallel")),
        name="win",
    )(q_rot, k_rot, k_rot, k_rot, proj, proj, proj)


def _merge_kernel(yg_ref, oc_ref, os_ref, ow_ref, ng_ref, ga_ref, gb_ref, wg_ref, wn_ref, o_ref, yn_scr):
    sg = jax.nn.sigmoid(ng_ref[...])
    for h in range(NSA_HEADS):
        sl = slice(h * NSA_DH, (h + 1) * NSA_DH)
        y = (sg[:, 3 * h:3 * h + 1] * oc_ref[:, sl] + sg[:, 3 * h + 1:3 * h + 2] * os_ref[:, sl]
             + sg[:, 3 * h + 2:3 * h + 3] * ow_ref[:, sl])
        yn_scr[:, sl] = y.astype(bf16)
    up_g = _dot(yg_ref[...].astype(bf16), wg_ref[...])
    up_n = _dot(yn_scr[...], wn_ref[...])
    o_ref[...] = (jax.nn.sigmoid(ga_ref[...]) * up_g + jax.nn.sigmoid(gb_ref[...]) * up_n).astype(o_ref.dtype)


def _merge(y_gla, o_cmp, o_slc, o_win, proj, wg, wn, tm):
    T = y_gla.shape[0]
    qw = NSA_HEADS * NSA_DH
    gvw = GLA_HEADS * GLA_DV
    const = dict(pipeline_mode=pl.Buffered(1))
    return pl.pallas_call(
        _merge_kernel,
        out_shape=jax.ShapeDtypeStruct((T, D_MODEL), bf16),
        grid=(T // tm,),
        in_specs=[
            pl.BlockSpec((tm, gvw), lambda i: (i, 0)),
            pl.BlockSpec((tm, qw), lambda i: (i, 0)),
            pl.BlockSpec((tm, qw), lambda i: (i, 0)),
            pl.BlockSpec((tm, qw), lambda i: (i, 0)),
            pl.BlockSpec((tm, LANES), lambda i: (i, OFF_NGATE // LANES)),
            pl.BlockSpec((tm, D_MODEL), lambda i: (i, OFF_GA // D_MODEL)),
            pl.BlockSpec((tm, D_MODEL), lambda i: (i, OFF_GB // D_MODEL)),
            pl.BlockSpec((gvw, D_MODEL), lambda i: (0, 0), **const),
            pl.BlockSpec((qw, D_MODEL), lambda i: (0, 0), **const),
        ],
        out_specs=pl.BlockSpec((tm, D_MODEL), lambda i: (i, 0)),
        scratch_shapes=[pltpu.VMEM((tm, qw), bf16)],
        compiler_params=_cp(("parallel",)),
        name="merge",
    )(y_gla, o_cmp, o_slc, o_win, proj, proj, proj, wg, wn)


def _outp_kernel(m_ref, x_ref, wo_ref, g2_ref, wq_ref, x1_ref, h2_ref, qp_ref):
    x1 = x_ref[...] + _dot(m_ref[...], wo_ref[...])
    x1_ref[...] = x1
    ms = jnp.mean(x1 * x1, axis=-1, keepdims=True)
    h2 = x1 * lax.rsqrt(ms + EPS) * g2_ref[...]
    h2_ref[...] = h2
    qp_ref[...] = _dot(h2.astype(bf16), wq_ref[...])


def _outp(merged, x2, wo, g2, wq, tm):
    T = x2.shape[0]
    pw = PEER_HEADS * PEER_DKEY
    const = dict(pipeline_mode=pl.Buffered(1))
    return pl.pallas_call(
        _outp_kernel,
        out_shape=(jax.ShapeDtypeStruct((T, D_MODEL), f32), jax.ShapeDtypeStruct((T, D_MODEL), f32),
                   jax.ShapeDtypeStruct((T, pw), f32)),
        grid=(T // tm,),
        in_specs=[
            pl.BlockSpec((tm, D_MODEL), lambda i: (i, 0)),
            pl.BlockSpec((tm, D_MODEL), lambda i: (i, 0)),
            pl.BlockSpec((D_MODEL, D_MODEL), lambda i: (0, 0), **const),
            pl.BlockSpec((1, D_MODEL), lambda i: (0, 0)),
            pl.BlockSpec((D_MODEL, pw), lambda i: (0, 0), **const),
        ],
        out_specs=(pl.BlockSpec((tm, D_MODEL), lambda i: (i, 0)), pl.BlockSpec((tm, D_MODEL), lambda i: (i, 0)),
                   pl.BlockSpec((tm, pw), lambda i: (i, 0))),
        compiler_params=_cp(("parallel",)),
        name="outp",
    )(merged, x2, wo, g2, wq)


def _topk_rows(cur, iota0, n, k, payload=None):
    vals, outs = [], []
    for _ in range(k):
        m = jnp.max(cur, axis=0, keepdims=True)
        first = jnp.min(jnp.where(cur == m, iota0, n), axis=0, keepdims=True)
        pick = iota0 == first
        vals.append(m)
        outs.append(first if payload is None else jnp.sum(jnp.where(pick, payload, 0), axis=0, keepdims=True))
        cur = jnp.where(pick, -jnp.inf, cur)
    return jnp.concatenate(vals, axis=0), jnp.concatenate(outs, axis=0)


def _ptopk_kernel(qp_ref, k1_ref, k2_ref, e_ref, g_ref, *, tt):
    K = PEER_TOPK
    half = PEER_DKEY // 2
    io_keys = lax.broadcasted_iota(i32, (PEER_NKEYS, tt), 0)
    io_cand = lax.broadcasted_iota(i32, (K * K, tt), 0)
    for h in range(PEER_HEADS):
        q1 = qp_ref[:, h * PEER_DKEY:h * PEER_DKEY + half]
        q2 = qp_ref[:, h * PEER_DKEY + half:(h + 1) * PEER_DKEY]
        s1 = _nt(k1_ref[h], q1, HI)
        s2 = _nt(k2_ref[h], q2, HI)
        v1, i1 = _topk_rows(s1, io_keys, PEER_NKEYS, K)
        v2, i2 = _topk_rows(s2, io_keys, PEER_NKEYS, K)
        cand = jnp.concatenate([v1[a:a + 1] + v2 for a in range(K)], axis=0)
        ecand = jnp.concatenate([i1[a:a + 1] * PEER_NKEYS + i2 for a in range(K)], axis=0)
        vals, experts = _topk_rows(cand, io_cand, K * K, K, payload=ecand)
        p = jnp.exp(vals - vals[0:1])
        g_ref[h] = p / jnp.sum(p, axis=0, keepdims=True)
        e_ref[h] = experts


def _ptopk(qp, k1, k2, tt):
    T = qp.shape[0]
    K = PEER_TOPK
    pw = PEER_HEADS * PEER_DKEY
    return pl.pallas_call(
        functools.partial(_ptopk_kernel, tt=tt),
        out_shape=(jax.ShapeDtypeStruct((PEER_HEADS, K, T), i32), jax.ShapeDtypeStruct((PEER_HEADS, K, T), f32)),
        grid=(T // tt,),
        in_specs=[
            pl.BlockSpec((tt, pw), lambda i: (i, 0)),
            pl.BlockSpec((PEER_HEADS, PEER_NKEYS, PEER_DKEY // 2), lambda i: (0, 0, 0)),
            pl.BlockSpec((PEER_HEADS, PEER_NKEYS, PEER_DKEY // 2), lambda i: (0, 0, 0)),
        ],
        out_specs=(pl.BlockSpec((PEER_HEADS, K, tt), lambda i: (0, 0, i)),
                   pl.BlockSpec((PEER_HEADS, K, tt), lambda i: (0, 0, i))),
        compiler_params=_cp(("parallel",)),
        name="ptopk",
    )(qp, k1, k2)


PEER_ROWS = 2 * D_MODEL // LANES
PEER_SEL = PEER_HEADS * PEER_TOPK


def _pexp_kernel(idx_ref, idx_next_ref, g_ref, x_ref, tab_ref, o_ref, buf, sem, *, tt):
    i = pl.program_id(0)
    n_steps = pl.num_programs(0)
    slot = lax.rem(i, 2)
    n_dma = tt * PEER_SEL
    urows = D_MODEL // LANES

    def issue(src_idx_ref, dst_slot):
        def body(n, carry):
            e = src_idx_ref[0, 0, n]
            pltpu.make_async_copy(tab_ref.at[e], buf.at[dst_slot, pl.ds(n * PEER_ROWS, PEER_ROWS), :],
                                  sem.at[dst_slot]).start()
            return carry
        lax.fori_loop(0, n_dma, body, 0, unroll=8)

    @pl.when(i == 0)
    def _():
        issue(idx_ref, 0)

    @pl.when(i + 1 < n_steps)
    def _():
        issue(idx_next_ref, 1 - slot)

    pltpu.make_async_copy(buf.at[slot], buf.at[slot], sem.at[slot]).wait()

    for t in range(tt):
        base = t * PEER_SEL * PEER_ROWS
        hid = jnp.zeros((1, PEER_SEL), f32)
        for s in range(urows):
            u = buf[slot, pl.ds(base + s, PEER_SEL, stride=PEER_ROWS), :]
            hid = hid + _nt(x_ref[t, s:s + 1, :].astype(bf16), u.astype(bf16))
        w = (g_ref[0, t:t + 1, :] * jax.nn.gelu(hid)).astype(bf16)
        for s in range(urows):
            v = buf[slot, pl.ds(base + urows + s, PEER_SEL, stride=PEER_ROWS), :]
            o_ref[t, s:s + 1, :] = _dot(w, v.astype(bf16))


def _pexp(idx3, gates, h2_3d, table, tt):
    n_steps = gates.shape[0]
    T = n_steps * tt
    urows = D_MODEL // LANES
    return pl.pallas_call(
        functools.partial(_pexp_kernel, tt=tt),
        out_shape=jax.ShapeDtypeStruct((T, urows, LANES), f32),
        grid=(n_steps,),
        in_specs=[
            pl.BlockSpec((1, 1, tt * PEER_SEL), lambda i: (i, 0, 0), memory_space=pltpu.SMEM),
            pl.BlockSpec((1, 1, tt * PEER_SEL), lambda i: (jnp.minimum(i + 1, n_steps - 1), 0, 0),
                         memory_space=pltpu.SMEM),
            pl.BlockSpec((1, tt, PEER_SEL), lambda i: (i, 0, 0)),
            pl.BlockSpec((tt, urows, LANES), lambda i: (i, 0, 0)),
            pl.BlockSpec(memory_space=pl.ANY),
        ],
        out_specs=pl.BlockSpec((tt, urows, LANES), lambda i: (i, 0, 0)),
        scratch_shapes=[pltpu.VMEM((2, tt * PEER_SEL * PEER_ROWS, LANES), f32), pltpu.SemaphoreType.DMA((2,))],
        compiler_params=_cp(("arbitrary",)),
        name="pexp",
    )(idx3, idx3, gates, h2_3d, table)


def _final_kernel(x_ref, p_ref, g_ref, o_ref):
    x = x_ref[...] + p_ref[...]
    ms = jnp.mean(x * x, axis=-1, keepdims=True)
    o_ref[...] = x * lax.rsqrt(ms + EPS) * g_ref[...]


def _final(x1, peer_out, gf, tm):
    T = x1.shape[0]
    return pl.pallas_call(
        _final_kernel,
        out_shape=jax.ShapeDtypeStruct((T, D_MODEL), f32),
        grid=(T // tm,),
        in_specs=[pl.BlockSpec((tm, D_MODEL), lambda i: (i, 0)), pl.BlockSpec((tm, D_MODEL), lambda i: (i, 0)),
                  pl.BlockSpec((1, D_MODEL), lambda i: (0, 0))],
        out_specs=pl.BlockSpec((tm, D_MODEL), lambda i: (i, 0)),
        compiler_params=_cp(("parallel",)),
        name="final",
    )(x1, peer_out, gf)


def _pack_w_in(w):
    zpad = lambda a, n: jnp.pad(a, ((0, 0), (0, n - a.shape[1])))
    return jnp.concatenate(
        [w[:, 0:5120], w[:, 5168:7216], w[:, 7232:12352], zpad(w[:, 5120:5168], LANES), zpad(w[:, 7216:7232], LANES)],
        axis=1).astype(bf16)


def _importance_map(n_cmp_pad, n_slc):
    ratio = SLC_BLOCK // CMP_STRIDE
    nrep = CMP_BLOCK // CMP_STRIDE
    at = np.zeros((n_slc, n_cmp_pad), np.float32)
    for j in range(n_slc):
        for m in range(ratio):
            for n in range(nrep):
                c = ratio * j + m - n
                if 0 <= c < n_cmp_pad:
                    at[j, c] += 1.0
    return jnp.asarray(at)


def _mixers(x2, positions, norm1_g, w_in, gla_w_decay, gla_b_decay, gla_norm_g, pos_k, pos_v, k_w1, k_w2, v_w1, v_w2,
            B, S):
    T = B * S
    G = NSA_GROUPS
    proj = _proj(x2, norm1_g.reshape(1, D_MODEL), _pack_w_in(w_in), tm=min(1024, T), tn=896)

    wd_pad = jnp.pad(gla_w_decay, ((0, LANES - GLA_LOWRANK), (0, 0)))
    y_gla = _gla(proj, wd_pad, gla_b_decay.reshape(1, -1), gla_norm_g.reshape(1, GLA_DV), B, S)

    inv = jnp.power(ROPE_THETA, -(jnp.arange(0, ROPE_DIM, 2, dtype=f32) / ROPE_DIM))
    inv_lane = jnp.concatenate([inv, inv, jnp.zeros((LANES - ROPE_DIM,), f32)]).reshape(1, LANES)
    q_rot, k_rot = _rope(proj, positions.reshape(T, 1), inv_lane, tm=min(512, T))

    n_rows = S // CMP_STRIDE
    kv_w = G * NSA_DH

    def stride_rows(a):
        return a.reshape(B, S, G, NSA_DH).transpose(0, 2, 1, 3).reshape(B * G, n_rows, CMP_STRIDE * NSA_DH)

    r_k = stride_rows(k_rot[:, 0:kv_w])
    r_v = stride_rows(proj[:, OFF_VCM:OFF_VCM + kv_w])
    pos2 = lambda p: p.reshape(2, (CMP_BLOCK // 2) * NSA_DH)
    kc = _compress(r_k, pos2(pos_k), k_w1.astype(bf16), k_w2.astype(bf16))
    vc = _compress(r_v, pos2(pos_v), v_w1.astype(bf16), v_w2.astype(bf16))

    n_slc = S // SLC_BLOCK
    o_cmp, sel_t = _cmp_sel(q_rot, kc, vc, _importance_map(n_rows, n_slc), B, S, tq=min(256, S))
    sel_tm = sel_t.transpose(0, 2, 1)
    o_slc = _slc(q_rot, k_rot, proj, sel_tm, B, S, tq=256, tk=min(512, S))
    o_win = _win(q_rot, k_rot, proj, B, S, tq=256)
    return proj, y_gla, o_cmp, o_slc, o_win


def _peer_stage(x1, h2, qp, sub_k1, sub_k2, peer_u, peer_v, final_g):
    T = x1.shape[0]
    e_t, g_t = _ptopk(qp, sub_k1, sub_k2, tt=256)
    tt = 4
    idx3 = e_t.transpose(2, 0, 1).reshape(T // tt, 1, tt * PEER_SEL)
    gates = g_t.transpose(2, 0, 1).reshape(T // tt, tt, PEER_SEL)
    urows = D_MODEL // LANES
    table = jnp.concatenate([peer_u.reshape(-1, urows, LANES), peer_v.reshape(-1, urows, LANES)], axis=1)
    peer_out = _pexp(idx3, gates, h2.reshape(T, urows, LANES), table, tt)
    return _final(x1, peer_out.reshape(T, D_MODEL), final_g.reshape(1, D_MODEL), tm=512)


def kernel(x, positions, norm1_g, w_in, gla_w_decay, gla_b_decay, gla_norm_g, nsa_cmp_pos_k, nsa_cmp_pos_v,
           nsa_cmp_k_w1, nsa_cmp_k_w2, nsa_cmp_v_w1, nsa_cmp_v_w2, w_up_gla, w_up_nsa, w_out, norm2_g,
           peer_w_q, peer_sub_k1, peer_sub_k2, peer_u, peer_v, final_g):
    B, S, D = x.shape
    T = B * S
    assert norm1_g.shape[0] == 1 and D == D_MODEL
    x2 = x.reshape(T, D)
    proj, y_gla, o_cmp, o_slc, o_win = _mixers(
        x2, positions, norm1_g[0], w_in[0], gla_w_decay[0], gla_b_decay[0], gla_norm_g[0], nsa_cmp_pos_k[0],
        nsa_cmp_pos_v[0], nsa_cmp_k_w1[0], nsa_cmp_k_w2[0], nsa_cmp_v_w1[0], nsa_cmp_v_w2[0], B, S)
    merged = _merge(y_gla, o_cmp, o_slc, o_win, proj, w_up_gla[0].astype(bf16), w_up_nsa[0].astype(bf16), tm=256)
    x1, h2, qp = _outp(merged, x2, w_out[0].astype(bf16), norm2_g[0].reshape(1, D), peer_w_q[0].astype(bf16), tm=256)
    out = _peer_stage(x1, h2, qp, peer_sub_k1[0], peer_sub_k2[0], peer_u[0], peer_v[0], final_g)
    return out.reshape(B, S, D)
```

```python
import functools

import jax
import jax.numpy as jnp
import numpy as np
from jax import lax
from jax.experimental import pallas as pl
from jax.experimental.pallas import tpu as pltpu

f32 = jnp.float32
bf16 = jnp.bfloat16
i32 = jnp.int32

D_MODEL = 2048
EPS = 1e-6
ROPE_THETA = 500000.0
GLA_HEADS = 4
GLA_DK = 128
GLA_DV = 256
GLA_LOWRANK = 16
GLA_TAU = 16.0
GLA_CHUNK = 64
GLA_SUB = 8
NSA_HEADS = 16
NSA_GROUPS = 4
NSA_HPG = 4
NSA_DH = 128
ROPE_DIM = 32
CMP_BLOCK = 32
CMP_STRIDE = 16
SLC_BLOCK = 64
SLC_TOPN = 16
WINDOW = 512
PEER_HEADS = 8
PEER_NKEYS = 128
PEER_DKEY = 128
PEER_TOPK = 16

LANES = 128
OFF_NQ = 0
OFF_KCM = 2048
OFF_VCM = 2560
OFF_KSL = 3072
OFF_VSL = 3584
OFF_KWN = 4096
OFF_VWN = 4608
OFF_GQ = 5120
OFF_GK = 5632
OFF_GV = 6144
OFF_GR = 7168
OFF_GA = 8192
OFF_GB = 10240
OFF_NGATE = 12288
OFF_GLR = 12416
NCOLS = 12544

VMEM_LIMIT = 56 * 1024 * 1024
NEG = -1e30
HI = lax.Precision.HIGHEST


def _cp(sem, vmem=VMEM_LIMIT):
    return pltpu.CompilerParams(dimension_semantics=sem, vmem_limit_bytes=vmem)


def _nt(a, b, precision=None):
    return lax.dot_general(a, b, (((1,), (1,)), ((), ())), preferred_element_type=f32, precision=precision)


def _tn(a, b, precision=None):
    return lax.dot_general(a, b, (((0,), (0,)), ((), ())), preferred_element_type=f32, precision=precision)


def _dot(a, b, precision=None):
    return jnp.dot(a, b, preferred_element_type=f32, precision=precision)


def _proj_kernel(x_ref, g_ref, w_ref, o_ref, h_scr):
    @pl.when(pl.program_id(1) == 0)
    def _():
        x = x_ref[...]
        ms = jnp.mean(x * x, axis=-1, keepdims=True)
        h_scr[...] = (x * lax.rsqrt(ms + EPS) * g_ref[...]).astype(bf16)

    o_ref[...] = _dot(h_scr[...], w_ref[...])


def _proj(x2, g1, w_pack, tm, tn):
    T = x2.shape[0]
    return pl.pallas_call(
        _proj_kernel,
        out_shape=jax.ShapeDtypeStruct((T, NCOLS), f32),
        grid=(T // tm, NCOLS // tn),
        in_specs=[
            pl.BlockSpec((tm, D_MODEL), lambda i, j: (i, 0)),
            pl.BlockSpec((1, D_MODEL), lambda i, j: (0, 0)),
            pl.BlockSpec((D_MODEL, tn), lambda i, j: (0, j)),
        ],
        out_specs=pl.BlockSpec((tm, tn), lambda i, j: (i, j)),
        scratch_shapes=[pltpu.VMEM((tm, D_MODEL), bf16)],
        compiler_params=_cp(("parallel", "arbitrary")),
        name="proj",
    )(x2, g1, w_pack)


def _gla_kernel(gq_ref, gk_ref, gv_ref, gr_ref, glr_ref, wd_ref, bd_ref, ng_ref, tri_ref, o_ref, st_ref):
    C = GLA_CHUNK
    R = GLA_SUB

    @pl.when(pl.program_id(1) == 0)
    def _():
        st_ref[...] = jnp.zeros_like(st_ref)

    glr = glr_ref[...]
    tri = tri_ref[...]
    lane = lax.broadcasted_iota(i32, (R, C), 1)
    row = lax.broadcasted_iota(i32, (R, C), 0)
    for h in range(GLA_HEADS):
        ks = slice(h * GLA_DK, (h + 1) * GLA_DK)
        vs = slice(h * GLA_DV, (h + 1) * GLA_DV)
        z = _dot(glr, wd_ref[:, ks], HI) + bd_ref[:, ks]
        g = (jnp.minimum(z, 0.0) - jnp.log(1.0 + jnp.exp(-jnp.abs(z)))) / GLA_TAU
        b = _dot(tri, g, HI)
        q = gq_ref[:, ks] * (GLA_DK ** -0.5)
        k = gk_ref[:, ks]
        v = gv_ref[:, vs].astype(bf16)
        st = st_ref[h]
        o = _nt((q * jnp.exp(b)).astype(bf16), st.astype(bf16))
        blocks = []
        for i in range(C // R):
            r0 = i * R
            bi = b[r0:r0 + R]
            ref = b[r0:r0 + 1]
            qi = q[r0:r0 + R]
            qt = qi * jnp.exp(bi - ref)
            kt = k * jnp.exp(jnp.minimum(ref - b, 0.0))
            a = _nt(qt.astype(bf16), kt.astype(bf16))
            a = jnp.where(lane < r0, a, 0.0)
            for s in range(R):
                w = qi * k[r0 + s:r0 + s + 1] * jnp.exp(jnp.minimum(bi - b[r0 + s:r0 + s + 1], 0.0))
                col = jnp.sum(w, axis=-1, keepdims=True)
                a = jnp.where(lane == r0 + s, jnp.where(row >= s, col, 0.0), a)
            blocks.append(a)
        attn = jnp.concatenate(blocks, axis=0)
        o = o + _dot(attn.astype(bf16), v)
        bl = b[C - 1:C]
        kd = k * jnp.exp(bl - b)
        st_ref[h] = jnp.exp(bl) * st + _tn(v, kd.astype(bf16))
        ms = jnp.mean(o * o, axis=-1, keepdims=True)
        y = o * lax.rsqrt(ms + EPS) * ng_ref[...]
        r = gr_ref[:, vs]
        o_ref[:, vs] = y * (r * jax.nn.sigmoid(r))


def _gla(proj, wd_pad, bd, ng, B, S):
    C = GLA_CHUNK
    nc = S // C
    tri = jnp.asarray(np.tril(np.ones((C, C), np.float32)))
    qk_w = GLA_HEADS * GLA_DK
    v_w = GLA_HEADS * GLA_DV
    row = lambda b, c: b * nc + c
    return pl.pallas_call(
        _gla_kernel,
        out_shape=jax.ShapeDtypeStruct((B * S, v_w), f32),
        grid=(B, nc),
        in_specs=[
            pl.BlockSpec((C, qk_w), lambda b, c: (row(b, c), OFF_GQ // qk_w)),
            pl.BlockSpec((C, qk_w), lambda b, c: (row(b, c), OFF_GK // qk_w)),
            pl.BlockSpec((C, v_w), lambda b, c: (row(b, c), OFF_GV // v_w)),
            pl.BlockSpec((C, v_w), lambda b, c: (row(b, c), OFF_GR // v_w)),
            pl.BlockSpec((C, LANES), lambda b, c: (row(b, c), OFF_GLR // LANES)),
            pl.BlockSpec((LANES, qk_w), lambda b, c: (0, 0)),
            pl.BlockSpec((1, qk_w), lambda b, c: (0, 0)),
            pl.BlockSpec((1, GLA_DV), lambda b, c: (0, 0)),
            pl.BlockSpec((C, C), lambda b, c: (0, 0)),
        ],
        out_specs=pl.BlockSpec((C, v_w), lambda b, c: (row(b, c), 0)),
        scratch_shapes=[pltpu.VMEM((GLA_HEADS, GLA_DV, GLA_DK), f32)],
        compiler_params=_cp(("parallel", "arbitrary")),
        name="gla",
    )(proj, proj, proj, proj, proj, wd_pad, bd, ng, tri)


def _rope_kernel(q_ref, kc_ref, ks_ref, kw_ref, pos_ref, inv_ref, qo_ref, ko_ref):
    ang = pos_ref[...].astype(f32) * inv_ref[...]
    cos = jnp.cos(ang)
    sin = jnp.sin(ang)
    lane = lax.broadcasted_iota(i32, ang.shape, 1)
    half = ROPE_DIM // 2
    s_up = jnp.where((lane >= half) & (lane < ROPE_DIM), sin, 0.0)
    s_dn = jnp.where(lane < half, -sin, 0.0)

    def rot(x):
        return x * cos + pltpu.roll(x, half, 1) * s_up + pltpu.roll(x, LANES - half, 1) * s_dn

    for h in range(NSA_HEADS):
        sl = slice(h * NSA_DH, (h + 1) * NSA_DH)
        qo_ref[:, sl] = rot(q_ref[:, sl])
    kv_w = NSA_GROUPS * NSA_DH
    for n, ref in enumerate((kc_ref, ks_ref, kw_ref)):
        for g in range(NSA_GROUPS):
            sl = slice(g * NSA_DH, (g + 1) * NSA_DH)
            ko_ref[:, n * kv_w + g * NSA_DH:n * kv_w + (g + 1) * NSA_DH] = rot(ref[:, sl])


def _rope(proj, pos_col, inv_lane, tm):
    T = proj.shape[0]
    q_w = NSA_HEADS * NSA_DH
    kv_w = NSA_GROUPS * NSA_DH
    return pl.pallas_call(
        _rope_kernel,
        out_shape=(jax.ShapeDtypeStruct((T, q_w), f32), jax.ShapeDtypeStruct((T, 3 * kv_w), f32)),
        grid=(T // tm,),
        in_specs=[
            pl.BlockSpec((tm, q_w), lambda i: (i, OFF_NQ // q_w)),
            pl.BlockSpec((tm, kv_w), lambda i: (i, OFF_KCM // kv_w)),
            pl.BlockSpec((tm, kv_w), lambda i: (i, OFF_KSL // kv_w)),
            pl.BlockSpec((tm, kv_w), lambda i: (i, OFF_KWN // kv_w)),
            pl.BlockSpec((tm, 1), lambda i: (i, 0)),
            pl.BlockSpec((1, LANES), lambda i: (0, 0)),
        ],
        out_specs=(pl.BlockSpec((tm, q_w), lambda i: (i, 0)), pl.BlockSpec((tm, 3 * kv_w), lambda i: (i, 0))),
        compiler_params=_cp(("parallel",)),
        name="rope",
    )(proj, proj, proj, proj, pos_col, inv_lane)


def _compress_kernel(r_ref, pos_ref, w1_ref, w2_ref, o_ref):
    half = (CMP_BLOCK // 2) * NSA_DH
    r = r_ref[0]
    n_rows = r.shape[0]
    ra = (r + pos_ref[0:1, :]).astype(bf16)
    rb = (pltpu.roll(r, n_rows - 1, 0) + pos_ref[1:2, :]).astype(bf16)
    hid = _dot(ra, w1_ref[0:half, :]) + _dot(rb, w1_ref[half:2 * half, :])
    o_ref[0] = _dot(jax.nn.gelu(hid).astype(bf16), w2_ref[...])


def _compress(r, pos2, w1, w2):
    BG, n_rows, width = r.shape
    return pl.pallas_call(
        _compress_kernel,
        out_shape=jax.ShapeDtypeStruct((BG, n_rows, NSA_DH), f32),
        grid=(BG,),
        in_specs=[
            pl.BlockSpec((1, n_rows, width), lambda i: (i, 0, 0)),
            pl.BlockSpec((2, width), lambda i: (0, 0)),
            pl.BlockSpec((2 * width, NSA_DH), lambda i: (0, 0)),
            pl.BlockSpec((NSA_DH, NSA_DH), lambda i: (0, 0)),
        ],
        out_specs=pl.BlockSpec((1, n_rows, NSA_DH), lambda i: (i, 0, 0)),
        compiler_params=_cp(("parallel",)),
        name="compress",
    )(r, pos2, w1, w2)


def _cmp_sel_kernel(q_ref, kc_ref, vc_ref, at_ref, o_ref, sel_ref, *, tq, n_slc):
    i = pl.program_id(2)
    ncp = kc_ref.shape[1]
    scale = NSA_DH ** -0.5
    t = i * tq + lax.broadcasted_iota(i32, (tq, ncp), 0)
    n_end = lax.broadcasted_iota(i32, (tq, ncp), 1) * CMP_STRIDE + (CMP_BLOCK - 1)
    cmask = n_end <= t
    kc = kc_ref[0].astype(bf16)
    vc = vc_ref[0].astype(bf16)
    pg = jnp.zeros((tq, ncp), f32)
    for hp in range(NSA_HPG):
        sl = slice(hp * NSA_DH, (hp + 1) * NSA_DH)
        s = _nt(q_ref[:, sl].astype(bf16), kc) * scale
        s = jnp.where(cmask, s, NEG)
        m = jnp.max(s, axis=-1, keepdims=True)
        p = jnp.where(cmask, jnp.exp(s - m), 0.0)
        p = p / jnp.maximum(jnp.sum(p, axis=-1, keepdims=True), 1e-30)
        o_ref[:, sl] = _dot(p.astype(bf16), vc)
        pg = pg + p
    imp = _nt(at_ref[...], pg, HI)
    j = lax.broadcasted_iota(i32, (n_slc, tq), 0)
    blk = (i * tq + lax.broadcasted_iota(i32, (n_slc, tq), 1)) // SLC_BLOCK
    forced = (j == 0) | (j == blk) | (j == blk - 1)
    imp = jnp.where(forced, 1e9, jnp.where(j > blk, -1e9, imp))
    sel = jnp.zeros((n_slc, tq), f32)
    cur = imp
    for _ in range(min(SLC_TOPN, n_slc)):
        m = jnp.max(cur, axis=0, keepdims=True)
        first = jnp.min(jnp.where(cur == m, j, n_slc), axis=0, keepdims=True)
        pick = j == first
        sel = jnp.where(pick, 1.0, sel)
        cur = jnp.where(pick, -jnp.inf, cur)
    sel_ref[0] = sel


def _cmp_sel(q_rot, kc, vc, at, B, S, tq):
    nq = S // tq
    G = NSA_GROUPS
    ncp = kc.shape[1]
    n_slc = S // SLC_BLOCK
    gw = NSA_HPG * NSA_DH
    return pl.pallas_call(
        functools.partial(_cmp_sel_kernel, tq=tq, n_slc=n_slc),
        out_shape=(jax.ShapeDtypeStruct((B * S, NSA_HEADS * NSA_DH), f32),
                   jax.ShapeDtypeStruct((B * G, n_slc, S), f32)),
        grid=(B, G, nq),
        in_specs=[
            pl.BlockSpec((tq, gw), lambda b, g, i: (b * nq + i, g)),
            pl.BlockSpec((1, ncp, NSA_DH), lambda b, g, i: (b * G + g, 0, 0)),
            pl.BlockSpec((1, ncp, NSA_DH), lambda b, g, i: (b * G + g, 0, 0)),
            pl.BlockSpec((n_slc, ncp), lambda b, g, i: (0, 0)),
        ],
        out_specs=(pl.BlockSpec((tq, gw), lambda b, g, i: (b * nq + i, g)),
                   pl.BlockSpec((1, n_slc, tq), lambda b, g, i: (b * G + g, 0, i))),
        compiler_params=_cp(("parallel", "parallel", "parallel")),
        name="cmp_sel",
    )(q_rot, kc, vc, at)


def _slc_kernel(q_ref, k_ref, v_ref, sel_ref, o_ref, m_scr, l_scr, acc_scr, *, tq, tk, n_slc):
    i = pl.program_id(2)
    j = pl.program_id(3)
    nk = pl.num_programs(3)
    scale = NSA_DH ** -0.5

    @pl.when(j == 0)
    def _():
        m_scr[...] = jnp.full_like(m_scr, NEG)
        l_scr[...] = jnp.zeros_like(l_scr)
        acc_scr[...] = jnp.zeros_like(acc_scr)

    @pl.when(j * tk < (i + 1) * tq)
    def _():
        kblk = (j * tk + lax.broadcasted_iota(i32, (n_slc, tk), 1)) // SLC_BLOCK
        expand = jnp.where(kblk == lax.broadcasted_iota(i32, (n_slc, tk), 0), 1.0, 0.0).astype(bf16)
        picked = _dot(sel_ref[0].astype(bf16), expand)
        t = i * tq + lax.broadcasted_iota(i32, (tq, tk), 0)
        kpos = j * tk + lax.broadcasted_iota(i32, (tq, tk), 1)
        mask = jnp.where(kpos <= t, picked, 0.0) > 0.5
        k = k_ref[...].astype(bf16)
        v = v_ref[...].astype(bf16)
        for hp in range(NSA_HPG):
            sl = slice(hp * NSA_DH, (hp + 1) * NSA_DH)
            s = _nt(q_ref[:, sl].astype(bf16), k) * scale
            s = jnp.where(mask, s, NEG)
            m_old = m_scr[hp]
            m_new = jnp.maximum(m_old, jnp.max(s, axis=-1, keepdims=True))
            alpha = jnp.exp(m_old - m_new)
            p = jnp.where(mask, jnp.exp(s - m_new), 0.0)
            l_scr[hp] = alpha * l_scr[hp] + jnp.sum(p, axis=-1, keepdims=True)
            acc_scr[hp] = alpha * acc_scr[hp] + _dot(p.astype(bf16), v)
            m_scr[hp] = m_new

    @pl.when(j == nk - 1)
    def _():
        for hp in range(NSA_HPG):
            sl = slice(hp * NSA_DH, (hp + 1) * NSA_DH)
            o_ref[:, sl] = acc_scr[hp] / jnp.maximum(l_scr[hp], 1e-30)


def _slc(q_rot, k_rot, proj, sel_tm, B, S, tq, tk):
    nq, nk = S // tq, S // tk
    G = NSA_GROUPS
    n_slc = S // SLC_BLOCK
    gw = NSA_HPG * NSA_DH
    kv_w = NSA_GROUPS * NSA_DH

    def kj(i, j):
        return jnp.minimum(j, ((i + 1) * tq - 1) // tk)

    return pl.pallas_call(
        functools.partial(_slc_kernel, tq=tq, tk=tk, n_slc=n_slc),
        out_shape=jax.ShapeDtypeStruct((B * S, NSA_HEADS * NSA_DH), f32),
        grid=(B, G, nq, nk),
        in_specs=[
            pl.BlockSpec((tq, gw), lambda b, g, i, j: (b * nq + i, g)),
            pl.BlockSpec((tk, NSA_DH), lambda b, g, i, j: (b * nk + kj(i, j), kv_w // NSA_DH + g)),
            pl.BlockSpec((tk, NSA_DH), lambda b, g, i, j: (b * nk + kj(i, j), OFF_VSL // NSA_DH + g)),
            pl.BlockSpec((1, tq, n_slc), lambda b, g, i, j: (b * G + g, i, 0)),
        ],
        out_specs=pl.BlockSpec((tq, gw), lambda b, g, i, j: (b * nq + i, g)),
        scratch_shapes=[pltpu.VMEM((NSA_HPG, tq, 1), f32), pltpu.VMEM((NSA_HPG, tq, 1), f32),
                        pltpu.VMEM((NSA_HPG, tq, NSA_DH), f32)],
        compiler_params=_cp(("parallel", "parallel", "parallel", "arbitrary")),
        name="slc",
    )(q_rot, k_rot, proj, sel_tm)


def _win_kernel(q_ref, k0_ref, k1_ref, k2_ref, v0_ref, v1_ref, v2_ref, o_ref, *, tq):
    i = pl.program_id(2)
    scale = NSA_DH ** -0.5
    nb = WINDOW // tq + 1
    k = jnp.concatenate([k0_ref[...], k1_ref[...], k2_ref[...]], axis=0).astype(bf16)
    v = jnp.concatenate([v0_ref[...], v1_ref[...], v2_ref[...]], axis=0).astype(bf16)
    t = i * tq + lax.broadcasted_iota(i32, (tq, nb * tq), 0)
    kpos = (i - (nb - 1)) * tq + lax.broadcasted_iota(i32, (tq, nb * tq), 1)
    mask = jnp.where(kpos >= 0, jnp.where(kpos <= t, jnp.where(kpos > t - WINDOW, 1.0, 0.0), 0.0), 0.0) > 0.5
    for hp in range(NSA_HPG):
        sl = slice(hp * NSA_DH, (hp + 1) * NSA_DH)
        s = _nt(q_ref[:, sl].astype(bf16), k) * scale
        s = jnp.where(mask, s, NEG)
        m = jnp.max(s, axis=-1, keepdims=True)
        p = jnp.where(mask, jnp.exp(s - m), 0.0)
        p = p / jnp.maximum(jnp.sum(p, axis=-1, keepdims=True), 1e-30)
        o_ref[:, sl] = _dot(p.astype(bf16), v)


def _win(q_rot, k_rot, proj, B, S, tq):
    assert WINDOW % tq == 0 and WINDOW // tq == 2
    nq = S // tq
    gw = NSA_HPG * NSA_DH
    kv_w = NSA_GROUPS * NSA_DH

    def kspec(d, col0):
        return pl.BlockSpec((tq, NSA_DH), lambda b, g, i: (b * nq + jnp.maximum(i - 2 + d, 0), col0 + g))

    return pl.pallas_call(
        functools.partial(_win_kernel, tq=tq),
        out_shape=jax.ShapeDtypeStruct((B * S, NSA_HEADS * NSA_DH), f32),
        grid=(B, NSA_GROUPS, nq),
        in_specs=[pl.BlockSpec((tq, gw), lambda b, g, i: (b * nq + i, g))]
        + [kspec(d, 2 * kv_w // NSA_DH) for d in range(3)]
        + [kspec(d, OFF_VWN // NSA_DH) for d in range(3)],
        out_specs=pl.BlockSpec((tq, gw), lambda b, g, i: (b * nq + i, g)),
        compiler_params=_cp(("parallel", "parallel", "parallel")),
        name="win",
    )(q_rot, k_rot, k_rot, k_rot, proj, proj, proj)


def _merge_kernel(yg_ref, oc_ref, os_ref, ow_ref, ng_ref, ga_ref, gb_ref, wg_ref, wn_ref, o_ref, yn_scr):
    sg = jax.nn.sigmoid(ng_ref[...])
    for h in range(NSA_HEADS):
        sl = slice(h * NSA_DH, (h + 1) * NSA_DH)
        y = (sg[:, 3 * h:3 * h + 1] * oc_ref[:, sl] + sg[:, 3 * h + 1:3 * h + 2] * os_ref[:, sl]
             + sg[:, 3 * h + 2:3 * h + 3] * ow_ref[:, sl])
        yn_scr[:, sl] = y.astype(bf16)
    up_g = _dot(yg_ref[...].astype(bf16), wg_ref[...])
    up_n = _dot(yn_scr[...], wn_ref[...])
    o_ref[...] = (jax.nn.sigmoid(ga_ref[...]) * up_g + jax.nn.sigmoid(gb_ref[...]) * up_n).astype(o_ref.dtype)


def _merge(y_gla, o_cmp, o_slc, o_win, proj, wg, wn, tm):
    T = y_gla.shape[0]
    qw = NSA_HEADS * NSA_DH
    gvw = GLA_HEADS * GLA_DV
    const = dict(pipeline_mode=pl.Buffered(1))
    return pl.pallas_call(
        _merge_kernel,
        out_shape=jax.ShapeDtypeStruct((T, D_MODEL), bf16),
        grid=(T // tm,),
        in_specs=[
            pl.BlockSpec((tm, gvw), lambda i: (i, 0)),
            pl.BlockSpec((tm, qw), lambda i: (i, 0)),
            pl.BlockSpec((tm, qw), lambda i: (i, 0)),
            pl.BlockSpec((tm, qw), lambda i: (i, 0)),
            pl.BlockSpec((tm, LANES), lambda i: (i, OFF_NGATE // LANES)),
            pl.BlockSpec((tm, D_MODEL), lambda i: (i, OFF_GA // D_MODEL)),
            pl.BlockSpec((tm, D_MODEL), lambda i: (i, OFF_GB // D_MODEL)),
            pl.BlockSpec((gvw, D_MODEL), lambda i: (0, 0), **const),
            pl.BlockSpec((qw, D_MODEL), lambda i: (0, 0), **const),
        ],
        out_specs=pl.BlockSpec((tm, D_MODEL), lambda i: (i, 0)),
        scratch_shapes=[pltpu.VMEM((tm, qw), bf16)],
        compiler_params=_cp(("parallel",)),
        name="merge",
    )(y_gla, o_cmp, o_slc, o_win, proj, proj, proj, wg, wn)


def _outp_kernel(m_ref, x_ref, wo_ref, g2_ref, wq_ref, x1_ref, h2_ref, qp_ref):
    x1 = x_ref[...] + _dot(m_ref[...], wo_ref[...])
    x1_ref[...] = x1
    ms = jnp.mean(x1 * x1, axis=-1, keepdims=True)
    h2 = x1 * lax.rsqrt(ms + EPS) * g2_ref[...]
    h2_ref[...] = h2
    qp_ref[...] = _dot(h2.astype(bf16), wq_ref[...])


def _outp(merged, x2, wo, g2, wq, tm):
    T = x2.shape[0]
    pw = PEER_HEADS * PEER_DKEY
    const = dict(pipeline_mode=pl.Buffered(1))
    return pl.pallas_call(
        _outp_kernel,
        out_shape=(jax.ShapeDtypeStruct((T, D_MODEL), f32), jax.ShapeDtypeStruct((T, D_MODEL), f32),
                   jax.ShapeDtypeStruct((T, pw), f32)),
        grid=(T // tm,),
        in_specs=[
            pl.BlockSpec((tm, D_MODEL), lambda i: (i, 0)),
            pl.BlockSpec((tm, D_MODEL), lambda i: (i, 0)),
            pl.BlockSpec((D_MODEL, D_MODEL), lambda i: (0, 0), **const),
            pl.BlockSpec((1, D_MODEL), lambda i: (0, 0)),
            pl.BlockSpec((D_MODEL, pw), lambda i: (0, 0), **const),
        ],
        out_specs=(pl.BlockSpec((tm, D_MODEL), lambda i: (i, 0)), pl.BlockSpec((tm, D_MODEL), lambda i: (i, 0)),
                   pl.BlockSpec((tm, pw), lambda i: (i, 0))),
        compiler_params=_cp(("parallel",)),
        name="outp",
    )(merged, x2, wo, g2, wq)


def _topk_rows(cur, iota0, n, k, payload=None):
    vals, outs = [], []
    for _ in range(k):
        m = jnp.max(cur, axis=0, keepdims=True)
        first = jnp.min(jnp.where(cur == m, iota0, n), axis=0, keepdims=True)
        pick = iota0 == first
        vals.append(m)
        outs.append(first if payload is None else jnp.sum(jnp.where(pick, payload, 0), axis=0, keepdims=True))
        cur = jnp.where(pick, -jnp.inf, cur)
    return jnp.concatenate(vals, axis=0), jnp.concatenate(outs, axis=0)


def _ptopk_kernel(qp_ref, k1_ref, k2_ref, e_ref, g_ref, *, tt):
    K = PEER_TOPK
    half = PEER_DKEY // 2
    io_keys = lax.broadcasted_iota(i32, (PEER_NKEYS, tt), 0)
    io_cand = lax.broadcasted_iota(i32, (K * K, tt), 0)
    for h in range(PEER_HEADS):
        q1 = qp_ref[:, h * PEER_DKEY:h * PEER_DKEY + half]
        q2 = qp_ref[:, h * PEER_DKEY + half:(h + 1) * PEER_DKEY]
        s1 = _nt(k1_ref[h], q1, HI)
        s2 = _nt(k2_ref[h], q2, HI)
        v1, i1 = _topk_rows(s1, io_keys, PEER_NKEYS, K)
        v2, i2 = _topk_rows(s2, io_keys, PEER_NKEYS, K)
        cand = jnp.concatenate([v1[a:a + 1] + v2 for a in range(K)], axis=0)
        ecand = jnp.concatenate([i1[a:a + 1] * PEER_NKEYS + i2 for a in range(K)], axis=0)
        vals, experts = _topk_rows(cand, io_cand, K * K, K, payload=ecand)
        p = jnp.exp(vals - vals[0:1])
        g_ref[h] = p / jnp.sum(p, axis=0, keepdims=True)
        e_ref[h] = experts


def _ptopk(qp, k1, k2, tt):
    T = qp.shape[0]
    K = PEER_TOPK
    pw = PEER_HEADS * PEER_DKEY
    return pl.pallas_call(
        functools.partial(_ptopk_kernel, tt=tt),
        out_shape=(jax.ShapeDtypeStruct((PEER_HEADS, K, T), i32), jax.ShapeDtypeStruct((PEER_HEADS, K, T), f32)),
        grid=(T // tt,),
        in_specs=[
            pl.BlockSpec((tt, pw), lambda i: (i, 0)),
            pl.BlockSpec((PEER_HEADS, PEER_NKEYS, PEER_DKEY // 2), lambda i: (0, 0, 0)),
            pl.BlockSpec((PEER_HEADS, PEER_NKEYS, PEER_DKEY // 2), lambda i: (0, 0, 0)),
        ],
        out_specs=(pl.BlockSpec((PEER_HEADS, K, tt), lambda i: (0, 0, i)),
                   pl.BlockSpec((PEER_HEADS, K, tt), lambda i: (0, 0, i))),
        compiler_params=_cp(("parallel",)),
        name="ptopk",
    )(qp, k1, k2)


PEER_ROWS = 2 * D_MODEL // LANES
PEER_SEL = PEER_HEADS * PEER_TOPK


def _pexp_kernel(idx_ref, idx_next_ref, g_ref, x_ref, tab_ref, o_ref, buf_a, buf_b, sem, *, tt):
    i = pl.program_id(0)
    n_steps = pl.num_programs(0)
    urows = D_MODEL // LANES

    def start(src_idx_ref, n, dst, dst_sem):
        e = src_idx_ref[0, 0, n]
        pltpu.make_async_copy(tab_ref.at[e], dst.at[:, n, :], dst_sem).start()

    def wait_all(dst, dst_sem):
        pltpu.make_async_copy(dst, dst, dst_sem).wait()

    def compute(src, t):
        rows = slice(t * PEER_SEL, (t + 1) * PEER_SEL)
        u = jnp.concatenate([src[s, rows, :].astype(bf16) for s in range(urows)], axis=1)
        hid = _nt(x_ref[0, t:t + 1, :].astype(bf16), u)
        w = (g_ref[0, t:t + 1, :] * jax.nn.gelu(hid)).astype(bf16)
        v = jnp.concatenate([src[urows + s, rows, :].astype(bf16) for s in range(urows)], axis=1)
        o_ref[0, t:t + 1, :] = _dot(w, v)

    def step(cur, cur_sem, nxt, nxt_sem):
        wait_all(cur, cur_sem)
        for t in range(tt):
            for k in range(PEER_SEL):
                start(idx_next_ref, t * PEER_SEL + k, nxt, nxt_sem)
            compute(cur, t)

    @pl.when(i == 0)
    def _():
        def body(n, carry):
            start(idx_ref, n, buf_a, sem.at[0])
            return carry
        lax.fori_loop(0, tt * PEER_SEL, body, 0, unroll=8)

    @pl.when(lax.rem(i, 2) == 0)
    def _():
        step(buf_a, sem.at[0], buf_b, sem.at[1])

    @pl.when(lax.rem(i, 2) == 1)
    def _():
        step(buf_b, sem.at[1], buf_a, sem.at[0])

    @pl.when((i == n_steps - 1) & (lax.rem(i, 2) == 0))
    def _():
        wait_all(buf_b, sem.at[1])

    @pl.when((i == n_steps - 1) & (lax.rem(i, 2) == 1))
    def _():
        wait_all(buf_a, sem.at[0])


def _pexp(idx3, gates, h2_3d, table, tt):
    n_steps = gates.shape[0]
    return pl.pallas_call(
        functools.partial(_pexp_kernel, tt=tt),
        out_shape=jax.ShapeDtypeStruct((n_steps, tt, D_MODEL), f32),
        grid=(n_steps,),
        in_specs=[
            pl.BlockSpec((1, 1, tt * PEER_SEL), lambda i: (i, 0, 0), memory_space=pltpu.SMEM),
            pl.BlockSpec((1, 1, tt * PEER_SEL), lambda i: (jnp.minimum(i + 1, n_steps - 1), 0, 0),
                         memory_space=pltpu.SMEM),
            pl.BlockSpec((1, tt, PEER_SEL), lambda i: (i, 0, 0)),
            pl.BlockSpec((1, tt, D_MODEL), lambda i: (i, 0, 0)),
            pl.BlockSpec(memory_space=pl.ANY),
        ],
        out_specs=pl.BlockSpec((1, tt, D_MODEL), lambda i: (i, 0, 0)),
        scratch_shapes=[pltpu.VMEM((PEER_ROWS, tt * PEER_SEL, LANES), f32),
                        pltpu.VMEM((PEER_ROWS, tt * PEER_SEL, LANES), f32), pltpu.SemaphoreType.DMA((2,))],
        compiler_params=_cp(("arbitrary",)),
        name="pexp",
    )(idx3, idx3, gates, h2_3d, table)


def _final_kernel(x_ref, p_ref, g_ref, o_ref):
    x = x_ref[...] + p_ref[...]
    ms = jnp.mean(x * x, axis=-1, keepdims=True)
    o_ref[...] = x * lax.rsqrt(ms + EPS) * g_ref[...]


def _final(x1, peer_out, gf, tm):
    T = x1.shape[0]
    return pl.pallas_call(
        _final_kernel,
        out_shape=jax.ShapeDtypeStruct((T, D_MODEL), f32),
        grid=(T // tm,),
        in_specs=[pl.BlockSpec((tm, D_MODEL), lambda i: (i, 0)), pl.BlockSpec((tm, D_MODEL), lambda i: (i, 0)),
                  pl.BlockSpec((1, D_MODEL), lambda i: (0, 0))],
        out_specs=pl.BlockSpec((tm, D_MODEL), lambda i: (i, 0)),
        compiler_params=_cp(("parallel",)),
        name="final",
    )(x1, peer_out, gf)


def _pack_w_in(w):
    zpad = lambda a, n: jnp.pad(a, ((0, 0), (0, n - a.shape[1])))
    return jnp.concatenate(
        [w[:, 0:5120], w[:, 5168:7216], w[:, 7232:12352], zpad(w[:, 5120:5168], LANES), zpad(w[:, 7216:7232], LANES)],
        axis=1).astype(bf16)


def _importance_map(n_cmp_pad, n_slc):
    ratio = SLC_BLOCK // CMP_STRIDE
    nrep = CMP_BLOCK // CMP_STRIDE
    at = np.zeros((n_slc, n_cmp_pad), np.float32)
    for j in range(n_slc):
        for m in range(ratio):
            for n in range(nrep):
                c = ratio * j + m - n
                if 0 <= c < n_cmp_pad:
                    at[j, c] += 1.0
    return jnp.asarray(at)


def _mixers(x2, positions, norm1_g, w_in, gla_w_decay, gla_b_decay, gla_norm_g, pos_k, pos_v, k_w1, k_w2, v_w1, v_w2,
            B, S):
    T = B * S
    G = NSA_GROUPS
    proj = _proj(x2, norm1_g.reshape(1, D_MODEL), _pack_w_in(w_in), tm=min(1024, T), tn=896)

    wd_pad = jnp.pad(gla_w_decay, ((0, LANES - GLA_LOWRANK), (0, 0)))
    y_gla = _gla(proj, wd_pad, gla_b_decay.reshape(1, -1), gla_norm_g.reshape(1, GLA_DV), B, S)

    inv = jnp.power(ROPE_THETA, -(jnp.arange(0, ROPE_DIM, 2, dtype=f32) / ROPE_DIM))
    inv_lane = jnp.concatenate([inv, inv, jnp.zeros((LANES - ROPE_DIM,), f32)]).reshape(1, LANES)
    q_rot, k_rot = _rope(proj, positions.reshape(T, 1), inv_lane, tm=min(512, T))

    n_rows = S // CMP_STRIDE
    kv_w = G * NSA_DH

    def stride_rows(a):
        return a.reshape(B, S, G, NSA_DH).transpose(0, 2, 1, 3).reshape(B * G, n_rows, CMP_STRIDE * NSA_DH)

    r_k = stride_rows(k_rot[:, 0:kv_w])
    r_v = stride_rows(proj[:, OFF_VCM:OFF_VCM + kv_w])
    pos2 = lambda p: p.reshape(2, (CMP_BLOCK // 2) * NSA_DH)
    kc = _compress(r_k, pos2(pos_k), k_w1.astype(bf16), k_w2.astype(bf16))
    vc = _compress(r_v, pos2(pos_v), v_w1.astype(bf16), v_w2.astype(bf16))

    n_slc = S // SLC_BLOCK
    o_cmp, sel_t = _cmp_sel(q_rot, kc, vc, _importance_map(n_rows, n_slc), B, S, tq=min(256, S))
    sel_tm = sel_t.transpose(0, 2, 1)
    o_slc = _slc(q_rot, k_rot, proj, sel_tm, B, S, tq=256, tk=min(512, S))
    o_win = _win(q_rot, k_rot, proj, B, S, tq=256)
    return proj, y_gla, o_cmp, o_slc, o_win


def _peer_stage(x1, h2, qp, sub_k1, sub_k2, peer_u, peer_v, final_g):
    T = x1.shape[0]
    e_t, g_t = _ptopk(qp, sub_k1, sub_k2, tt=256)
    tt = 4
    idx3 = e_t.transpose(2, 0, 1).reshape(T // tt, 1, tt * PEER_SEL)
    gates = g_t.transpose(2, 0, 1).reshape(T // tt, tt, PEER_SEL)
    urows = D_MODEL // LANES
    table = jnp.concatenate([peer_u.reshape(-1, urows, LANES), peer_v.reshape(-1, urows, LANES)], axis=1)
    peer_out = _pexp(idx3, gates, h2.reshape(T // tt, tt, D_MODEL), table, tt)
    return _final(x1, peer_out.reshape(T, D_MODEL), final_g.reshape(1, D_MODEL), tm=512)


def kernel(x, positions, norm1_g, w_in, gla_w_decay, gla_b_decay, gla_norm_g, nsa_cmp_pos_k, nsa_cmp_pos_v,
           nsa_cmp_k_w1, nsa_cmp_k_w2, nsa_cmp_v_w1, nsa_cmp_v_w2, w_up_gla, w_up_nsa, w_out, norm2_g,
           peer_w_q, peer_sub_k1, peer_sub_k2, peer_u, peer_v, final_g):
    B, S, D = x.shape
    T = B * S
    assert norm1_g.shape[0] == 1 and D == D_MODEL
    x2 = x.reshape(T, D)
    proj, y_gla, o_cmp, o_slc, o_win = _mixers(
        x2, positions, norm1_g[0], w_in[0], gla_w_decay[0], gla_b_decay[0], gla_norm_g[0], nsa_cmp_pos_k[0],
        nsa_cmp_pos_v[0], nsa_cmp_k_w1[0], nsa_cmp_k_w2[0], nsa_cmp_v_w1[0], nsa_cmp_v_w2[0], B, S)
    merged = _merge(y_gla, o_cmp, o_slc, o_win, proj, w_up_gla[0].astype(bf16), w_up_nsa[0].astype(bf16), tm=256)
    x1, h2, qp = _outp(merged, x2, w_out[0].astype(bf16), norm2_g[0].reshape(1, D), peer_w_q[0].astype(bf16), tm=256)
    out = _peer_stage(x1, h2, qp, peer_sub_k1[0], peer_sub_k2[0], peer_u[0], peer_v[0], final_g)
    return out.reshape(B, S, D)
```

```python
import functools

import jax
import jax.numpy as jnp
import numpy as np
from jax import lax
from jax.experimental import pallas as pl
from jax.experimental.pallas import tpu as pltpu

f32 = jnp.float32
bf16 = jnp.bfloat16
i32 = jnp.int32

D_MODEL = 2048
EPS = 1e-6
ROPE_THETA = 500000.0
GLA_HEADS = 4
GLA_DK = 128
GLA_DV = 256
GLA_LOWRANK = 16
GLA_TAU = 16.0
GLA_CHUNK = 64
GLA_SUB = 8
NSA_HEADS = 16
NSA_GROUPS = 4
NSA_HPG = 4
NSA_DH = 128
ROPE_DIM = 32
CMP_BLOCK = 32
CMP_STRIDE = 16
SLC_BLOCK = 64
SLC_TOPN = 16
WINDOW = 512
PEER_HEADS = 8
PEER_NKEYS = 128
PEER_DKEY = 128
PEER_TOPK = 16

LANES = 128
OFF_NQ = 0
OFF_KCM = 2048
OFF_VCM = 2560
OFF_KSL = 3072
OFF_VSL = 3584
OFF_KWN = 4096
OFF_VWN = 4608
OFF_GQ = 5120
OFF_GK = 5632
OFF_GV = 6144
OFF_GR = 7168
OFF_GA = 8192
OFF_GB = 10240
OFF_NGATE = 12288
OFF_GLR = 12416
NCOLS = 12544

VMEM_LIMIT = 56 * 1024 * 1024
NEG = -1e30
HI = lax.Precision.HIGHEST


def _cp(sem, vmem=VMEM_LIMIT):
    return pltpu.CompilerParams(dimension_semantics=sem, vmem_limit_bytes=vmem)


def _nt(a, b, precision=None):
    return lax.dot_general(a, b, (((1,), (1,)), ((), ())), preferred_element_type=f32, precision=precision)


def _tn(a, b, precision=None):
    return lax.dot_general(a, b, (((0,), (0,)), ((), ())), preferred_element_type=f32, precision=precision)


def _dot(a, b, precision=None):
    return jnp.dot(a, b, preferred_element_type=f32, precision=precision)


def _proj_kernel(x_ref, g_ref, w_ref, o_ref, h_scr):
    @pl.when(pl.program_id(1) == 0)
    def _():
        x = x_ref[...]
        ms = jnp.mean(x * x, axis=-1, keepdims=True)
        h_scr[...] = (x * lax.rsqrt(ms + EPS) * g_ref[...]).astype(bf16)

    o_ref[...] = _dot(h_scr[...], w_ref[...])


def _proj(x2, g1, w_pack, tm, tn):
    T = x2.shape[0]
    return pl.pallas_call(
        _proj_kernel,
        out_shape=jax.ShapeDtypeStruct((T, NCOLS), f32),
        grid=(T // tm, NCOLS // tn),
        in_specs=[
            pl.BlockSpec((tm, D_MODEL), lambda i, j: (i, 0)),
            pl.BlockSpec((1, D_MODEL), lambda i, j: (0, 0)),
            pl.BlockSpec((D_MODEL, tn), lambda i, j: (0, j)),
        ],
        out_specs=pl.BlockSpec((tm, tn), lambda i, j: (i, j)),
        scratch_shapes=[pltpu.VMEM((tm, D_MODEL), bf16)],
        compiler_params=_cp(("parallel", "arbitrary")),
        name="proj",
    )(x2, g1, w_pack)


def _gla_kernel(gq_ref, gk_ref, gv_ref, gr_ref, glr_ref, wd_ref, bd_ref, ng_ref, tri_ref, o_ref, st_ref):
    C = GLA_CHUNK
    R = GLA_SUB

    @pl.when(pl.program_id(1) == 0)
    def _():
        st_ref[...] = jnp.zeros_like(st_ref)

    glr = glr_ref[...]
    tri = tri_ref[...]
    lane = lax.broadcasted_iota(i32, (R, C), 1)
    row = lax.broadcasted_iota(i32, (R, C), 0)
    for h in range(GLA_HEADS):
        ks = slice(h * GLA_DK, (h + 1) * GLA_DK)
        vs = slice(h * GLA_DV, (h + 1) * GLA_DV)
        z = _dot(glr, wd_ref[:, ks], HI) + bd_ref[:, ks]
        g = (jnp.minimum(z, 0.0) - jnp.log(1.0 + jnp.exp(-jnp.abs(z)))) / GLA_TAU
        b = _dot(tri, g, HI)
        q = gq_ref[:, ks] * (GLA_DK ** -0.5)
        k = gk_ref[:, ks]
        v = gv_ref[:, vs].astype(bf16)
        st = st_ref[h]
        o = _nt((q * jnp.exp(b)).astype(bf16), st.astype(bf16))
        blocks = []
        for i in range(C // R):
            r0 = i * R
            bi = b[r0:r0 + R]
            ref = b[r0:r0 + 1]
            qi = q[r0:r0 + R]
            qt = qi * jnp.exp(bi - ref)
            kt = k * jnp.exp(jnp.minimum(ref - b, 0.0))
            a = _nt(qt.astype(bf16), kt.astype(bf16))
            a = jnp.where(lane < r0, a, 0.0)
            for s in range(R):
                w = qi * k[r0 + s:r0 + s + 1] * jnp.exp(jnp.minimum(bi - b[r0 + s:r0 + s + 1], 0.0))
                col = jnp.sum(w, axis=-1, keepdims=True)
                a = jnp.where(lane == r0 + s, jnp.where(row >= s, col, 0.0), a)
            blocks.append(a)
        attn = jnp.concatenate(blocks, axis=0)
        o = o + _dot(attn.astype(bf16), v)
        bl = b[C - 1:C]
        kd = k * jnp.exp(bl - b)
        st_ref[h] = jnp.exp(bl) * st + _tn(v, kd.astype(bf16))
        ms = jnp.mean(o * o, axis=-1, keepdims=True)
        y = o * lax.rsqrt(ms + EPS) * ng_ref[...]
        r = gr_ref[:, vs]
        o_ref[:, vs] = y * (r * jax.nn.sigmoid(r))


def _gla(proj, wd_pad, bd, ng, B, S):
    C = GLA_CHUNK
    nc = S // C
    tri = jnp.asarray(np.tril(np.ones((C, C), np.float32)))
    qk_w = GLA_HEADS * GLA_DK
    v_w = GLA_HEADS * GLA_DV
    row = lambda b, c: b * nc + c
    return pl.pallas_call(
        _gla_kernel,
        out_shape=jax.ShapeDtypeStruct((B * S, v_w), f32),
        grid=(B, nc),
        in_specs=[
            pl.BlockSpec((C, qk_w), lambda b, c: (row(b, c), OFF_GQ // qk_w)),
            pl.BlockSpec((C, qk_w), lambda b, c: (row(b, c), OFF_GK // qk_w)),
            pl.BlockSpec((C, v_w), lambda b, c: (row(b, c), OFF_GV // v_w)),
            pl.BlockSpec((C, v_w), lambda b, c: (row(b, c), OFF_GR // v_w)),
            pl.BlockSpec((C, LANES), lambda b, c: (row(b, c), OFF_GLR // LANES)),
            pl.BlockSpec((LANES, qk_w), lambda b, c: (0, 0)),
            pl.BlockSpec((1, qk_w), lambda b, c: (0, 0)),
            pl.BlockSpec((1, GLA_DV), lambda b, c: (0, 0)),
            pl.BlockSpec((C, C), lambda b, c: (0, 0)),
        ],
        out_specs=pl.BlockSpec((C, v_w), lambda b, c: (row(b, c), 0)),
        scratch_shapes=[pltpu.VMEM((GLA_HEADS, GLA_DV, GLA_DK), f32)],
        compiler_params=_cp(("parallel", "arbitrary")),
        name="gla",
    )(proj, proj, proj, proj, proj, wd_pad, bd, ng, tri)


def _rope_kernel(q_ref, kc_ref, ks_ref, kw_ref, pos_ref, inv_ref, qo_ref, ko_ref):
    ang = pos_ref[...].astype(f32) * inv_ref[...]
    cos = jnp.cos(ang)
    sin = jnp.sin(ang)
    lane = lax.broadcasted_iota(i32, ang.shape, 1)
    half = ROPE_DIM // 2
    s_up = jnp.where((lane >= half) & (lane < ROPE_DIM), sin, 0.0)
    s_dn = jnp.where(lane < half, -sin, 0.0)

    def rot(x):
        return x * cos + pltpu.roll(x, half, 1) * s_up + pltpu.roll(x, LANES - half, 1) * s_dn

    for h in range(NSA_HEADS):
        sl = slice(h * NSA_DH, (h + 1) * NSA_DH)
        qo_ref[:, sl] = rot(q_ref[:, sl])
    kv_w = NSA_GROUPS * NSA_DH
    for n, ref in enumerate((kc_ref, ks_ref, kw_ref)):
        for g in range(NSA_GROUPS):
            sl = slice(g * NSA_DH, (g + 1) * NSA_DH)
            ko_ref[:, n * kv_w + g * NSA_DH:n * kv_w + (g + 1) * NSA_DH] = rot(ref[:, sl])


def _rope(proj, pos_col, inv_lane, tm):
    T = proj.shape[0]
    q_w = NSA_HEADS * NSA_DH
    kv_w = NSA_GROUPS * NSA_DH
    return pl.pallas_call(
        _rope_kernel,
        out_shape=(jax.ShapeDtypeStruct((T, q_w), f32), jax.ShapeDtypeStruct((T, 3 * kv_w), f32)),
        grid=(T // tm,),
        in_specs=[
            pl.BlockSpec((tm, q_w), lambda i: (i, OFF_NQ // q_w)),
            pl.BlockSpec((tm, kv_w), lambda i: (i, OFF_KCM // kv_w)),
            pl.BlockSpec((tm, kv_w), lambda i: (i, OFF_KSL // kv_w)),
            pl.BlockSpec((tm, kv_w), lambda i: (i, OFF_KWN // kv_w)),
            pl.BlockSpec((tm, 1), lambda i: (i, 0)),
            pl.BlockSpec((1, LANES), lambda i: (0, 0)),
        ],
        out_specs=(pl.BlockSpec((tm, q_w), lambda i: (i, 0)), pl.BlockSpec((tm, 3 * kv_w), lambda i: (i, 0))),
        compiler_params=_cp(("parallel",)),
        name="rope",
    )(proj, proj, proj, proj, pos_col, inv_lane)


def _compress_kernel(r_ref, pos_ref, w1_ref, w2_ref, o_ref):
    half = (CMP_BLOCK // 2) * NSA_DH
    r = r_ref[0]
    n_rows = r.shape[0]
    ra = (r + pos_ref[0:1, :]).astype(bf16)
    rb = (pltpu.roll(r, n_rows - 1, 0) + pos_ref[1:2, :]).astype(bf16)
    hid = _dot(ra, w1_ref[0:half, :]) + _dot(rb, w1_ref[half:2 * half, :])
    o_ref[0] = _dot(jax.nn.gelu(hid).astype(bf16), w2_ref[...])


def _compress(r, pos2, w1, w2):
    BG, n_rows, width = r.shape
    return pl.pallas_call(
        _compress_kernel,
        out_shape=jax.ShapeDtypeStruct((BG, n_rows, NSA_DH), f32),
        grid=(BG,),
        in_specs=[
            pl.BlockSpec((1, n_rows, width), lambda i: (i, 0, 0)),
            pl.BlockSpec((2, width), lambda i: (0, 0)),
            pl.BlockSpec((2 * width, NSA_DH), lambda i: (0, 0)),
            pl.BlockSpec((NSA_DH, NSA_DH), lambda i: (0, 0)),
        ],
        out_specs=pl.BlockSpec((1, n_rows, NSA_DH), lambda i: (i, 0, 0)),
        compiler_params=_cp(("parallel",)),
        name="compress",
    )(r, pos2, w1, w2)


def _cmp_sel_kernel(q_ref, kc_ref, vc_ref, at_ref, o_ref, sel_ref, *, tq, n_slc):
    i = pl.program_id(2)
    ncp = kc_ref.shape[1]
    scale = NSA_DH ** -0.5
    t = i * tq + lax.broadcasted_iota(i32, (tq, ncp), 0)
    n_end = lax.broadcasted_iota(i32, (tq, ncp), 1) * CMP_STRIDE + (CMP_BLOCK - 1)
    cmask = n_end <= t
    kc = kc_ref[0].astype(bf16)
    vc = vc_ref[0].astype(bf16)
    pg = jnp.zeros((tq, ncp), f32)
    for hp in range(NSA_HPG):
        sl = slice(hp * NSA_DH, (hp + 1) * NSA_DH)
        s = _nt(q_ref[:, sl].astype(bf16), kc) * scale
        s = jnp.where(cmask, s, NEG)
        m = jnp.max(s, axis=-1, keepdims=True)
        p = jnp.where(cmask, jnp.exp(s - m), 0.0)
        p = p / jnp.maximum(jnp.sum(p, axis=-1, keepdims=True), 1e-30)
        o_ref[:, sl] = _dot(p.astype(bf16), vc)
        pg = pg + p
    imp = _nt(at_ref[...], pg, HI)
    j = lax.broadcasted_iota(i32, (n_slc, tq), 0)
    blk = (i * tq + lax.broadcasted_iota(i32, (n_slc, tq), 1)) // SLC_BLOCK
    forced = (j == 0) | (j == blk) | (j == blk - 1)
    imp = jnp.where(forced, 1e9, jnp.where(j > blk, -1e9, imp))
    sel = jnp.zeros((n_slc, tq), f32)
    cur = imp
    for _ in range(min(SLC_TOPN, n_slc)):
        m = jnp.max(cur, axis=0, keepdims=True)
        first = jnp.min(jnp.where(cur == m, j, n_slc), axis=0, keepdims=True)
        pick = j == first
        sel = jnp.where(pick, 1.0, sel)
        cur = jnp.where(pick, -jnp.inf, cur)
    sel_ref[0] = sel


def _cmp_sel(q_rot, kc, vc, at, B, S, tq):
    nq = S // tq
    G = NSA_GROUPS
    ncp = kc.shape[1]
    n_slc = S // SLC_BLOCK
    gw = NSA_HPG * NSA_DH
    return pl.pallas_call(
        functools.partial(_cmp_sel_kernel, tq=tq, n_slc=n_slc),
        out_shape=(jax.ShapeDtypeStruct((B * S, NSA_HEADS * NSA_DH), f32),
                   jax.ShapeDtypeStruct((B * G, n_slc, S), f32)),
        grid=(B, G, nq),
        in_specs=[
            pl.BlockSpec((tq, gw), lambda b, g, i: (b * nq + i, g)),
            pl.BlockSpec((1, ncp, NSA_DH), lambda b, g, i: (b * G + g, 0, 0)),
            pl.BlockSpec((1, ncp, NSA_DH), lambda b, g, i: (b * G + g, 0, 0)),
            pl.BlockSpec((n_slc, ncp), lambda b, g, i: (0, 0)),
        ],
        out_specs=(pl.BlockSpec((tq, gw), lambda b, g, i: (b * nq + i, g)),
                   pl.BlockSpec((1, n_slc, tq), lambda b, g, i: (b * G + g, 0, i))),
        compiler_params=_cp(("parallel", "parallel", "parallel")),
        name="cmp_sel",
    )(q_rot, kc, vc, at)


def _slc_kernel(q_ref, k_ref, v_ref, sel_ref, o_ref, m_scr, l_scr, acc_scr, *, tq, tk, n_slc):
    i = pl.program_id(2)
    j = pl.program_id(3)
    nk = pl.num_programs(3)
    scale = NSA_DH ** -0.5

    @pl.when(j == 0)
    def _():
        m_scr[...] = jnp.full_like(m_scr, NEG)
        l_scr[...] = jnp.zeros_like(l_scr)
        acc_scr[...] = jnp.zeros_like(acc_scr)

    @pl.when(j * tk < (i + 1) * tq)
    def _():
        kblk = (j * tk + lax.broadcasted_iota(i32, (n_slc, tk), 1)) // SLC_BLOCK
        expand = jnp.where(kblk == lax.broadcasted_iota(i32, (n_slc, tk), 0), 1.0, 0.0).astype(bf16)
        picked = _dot(sel_ref[0].astype(bf16), expand)
        t = i * tq + lax.broadcasted_iota(i32, (tq, tk), 0)
        kpos = j * tk + lax.broadcasted_iota(i32, (tq, tk), 1)
        mask = jnp.where(kpos <= t, picked, 0.0) > 0.5
        k = k_ref[...].astype(bf16)
        v = v_ref[...].astype(bf16)
        for hp in range(NSA_HPG):
            sl = slice(hp * NSA_DH, (hp + 1) * NSA_DH)
            s = _nt(q_ref[:, sl].astype(bf16), k) * scale
            s = jnp.where(mask, s, NEG)
            m_old = m_scr[hp]
            m_new = jnp.maximum(m_old, jnp.max(s, axis=-1, keepdims=True))
            alpha = jnp.exp(m_old - m_new)
            p = jnp.where(mask, jnp.exp(s - m_new), 0.0)
            l_scr[hp] = alpha * l_scr[hp] + jnp.sum(p, axis=-1, keepdims=True)
            acc_scr[hp] = alpha * acc_scr[hp] + _dot(p.astype(bf16), v)
            m_scr[hp] = m_new

    @pl.when(j == nk - 1)
    def _():
        for hp in range(NSA_HPG):
            sl = slice(hp * NSA_DH, (hp + 1) * NSA_DH)
            o_ref[:, sl] = acc_scr[hp] / jnp.maximum(l_scr[hp], 1e-30)


def _slc(q_rot, k_rot, proj, sel_tm, B, S, tq, tk):
    nq, nk = S // tq, S // tk
    G = NSA_GROUPS
    n_slc = S // SLC_BLOCK
    gw = NSA_HPG * NSA_DH
    kv_w = NSA_GROUPS * NSA_DH

    def kj(i, j):
        return jnp.minimum(j, ((i + 1) * tq - 1) // tk)

    return pl.pallas_call(
        functools.partial(_slc_kernel, tq=tq, tk=tk, n_slc=n_slc),
        out_shape=jax.ShapeDtypeStruct((B * S, NSA_HEADS * NSA_DH), f32),
        grid=(B, G, nq, nk),
        in_specs=[
            pl.BlockSpec((tq, gw), lambda b, g, i, j: (b * nq + i, g)),
            pl.BlockSpec((tk, NSA_DH), lambda b, g, i, j: (b * nk + kj(i, j), kv_w // NSA_DH + g)),
            pl.BlockSpec((tk, NSA_DH), lambda b, g, i, j: (b * nk + kj(i, j), OFF_VSL // NSA_DH + g)),
            pl.BlockSpec((1, tq, n_slc), lambda b, g, i, j: (b * G + g, i, 0)),
        ],
        out_specs=pl.BlockSpec((tq, gw), lambda b, g, i, j: (b * nq + i, g)),
        scratch_shapes=[pltpu.VMEM((NSA_HPG, tq, 1), f32), pltpu.VMEM((NSA_HPG, tq, 1), f32),
                        pltpu.VMEM((NSA_HPG, tq, NSA_DH), f32)],
        compiler_params=_cp(("parallel", "parallel", "parallel", "arbitrary")),
        name="slc",
    )(q_rot, k_rot, proj, sel_tm)


def _win_kernel(q_ref, k0_ref, k1_ref, k2_ref, v0_ref, v1_ref, v2_ref, o_ref, *, tq):
    i = pl.program_id(2)
    scale = NSA_DH ** -0.5
    nb = WINDOW // tq + 1
    k = jnp.concatenate([k0_ref[...], k1_ref[...], k2_ref[...]], axis=0).astype(bf16)
    v = jnp.concatenate([v0_ref[...], v1_ref[...], v2_ref[...]], axis=0).astype(bf16)
    t = i * tq + lax.broadcasted_iota(i32, (tq, nb * tq), 0)
    kpos = (i - (nb - 1)) * tq + lax.broadcasted_iota(i32, (tq, nb * tq), 1)
    mask = jnp.where(kpos >= 0, jnp.where(kpos <= t, jnp.where(kpos > t - WINDOW, 1.0, 0.0), 0.0), 0.0) > 0.5
    for hp in range(NSA_HPG):
        sl = slice(hp * NSA_DH, (hp + 1) * NSA_DH)
        s = _nt(q_ref[:, sl].astype(bf16), k) * scale
        s = jnp.where(mask, s, NEG)
        m = jnp.max(s, axis=-1, keepdims=True)
        p = jnp.where(mask, jnp.exp(s - m), 0.0)
        p = p / jnp.maximum(jnp.sum(p, axis=-1, keepdims=True), 1e-30)
        o_ref[:, sl] = _dot(p.astype(bf16), v)


def _win(q_rot, k_rot, proj, B, S, tq):
    assert WINDOW % tq == 0 and WINDOW // tq == 2
    nq = S // tq
    gw = NSA_HPG * NSA_DH
    kv_w = NSA_GROUPS * NSA_DH

    def kspec(d, col0):
        return pl.BlockSpec((tq, NSA_DH), lambda b, g, i: (b * nq + jnp.maximum(i - 2 + d, 0), col0 + g))

    return pl.pallas_call(
        functools.partial(_win_kernel, tq=tq),
        out_shape=jax.ShapeDtypeStruct((B * S, NSA_HEADS * NSA_DH), f32),
        grid=(B, NSA_GROUPS, nq),
        in_specs=[pl.BlockSpec((tq, gw), lambda b, g, i: (b * nq + i, g))]
        + [kspec(d, 2 * kv_w // NSA_DH) for d in range(3)]
        + [kspec(d, OFF_VWN // NSA_DH) for d in range(3)],
        out_specs=pl.BlockSpec((tq, gw), lambda b, g, i: (b * nq + i, g)),
        compiler_params=_cp(("parallel", "parallel", "parallel")),
        name="win",
    )(q_rot, k_rot, k_rot, k_rot, proj, proj, proj)


def _merge_kernel(yg_ref, oc_ref, os_ref, ow_ref, ng_ref, ga_ref, gb_ref, wg_ref, wn_ref, o_ref, yn_scr):
    sg = jax.nn.sigmoid(ng_ref[...])
    for h in range(NSA_HEADS):
        sl = slice(h * NSA_DH, (h + 1) * NSA_DH)
        y = (sg[:, 3 * h:3 * h + 1] * oc_ref[:, sl] + sg[:, 3 * h + 1:3 * h + 2] * os_ref[:, sl]
             + sg[:, 3 * h + 2:3 * h + 3] * ow_ref[:, sl])
        yn_scr[:, sl] = y.astype(bf16)
    up_g = _dot(yg_ref[...].astype(bf16), wg_ref[...])
    up_n = _dot(yn_scr[...], wn_ref[...])
    o_ref[...] = (jax.nn.sigmoid(ga_ref[...]) * up_g + jax.nn.sigmoid(gb_ref[...]) * up_n).astype(o_ref.dtype)


def _merge(y_gla, o_cmp, o_slc, o_win, proj, wg, wn, tm):
    T = y_gla.shape[0]
    qw = NSA_HEADS * NSA_DH
    gvw = GLA_HEADS * GLA_DV
    const = dict(pipeline_mode=pl.Buffered(1))
    return pl.pallas_call(
        _merge_kernel,
        out_shape=jax.ShapeDtypeStruct((T, D_MODEL), bf16),
        grid=(T // tm,),
        in_specs=[
            pl.BlockSpec((tm, gvw), lambda i: (i, 0)),
            pl.BlockSpec((tm, qw), lambda i: (i, 0)),
            pl.BlockSpec((tm, qw), lambda i: (i, 0)),
            pl.BlockSpec((tm, qw), lambda i: (i, 0)),
            pl.BlockSpec((tm, LANES), lambda i: (i, OFF_NGATE // LANES)),
            pl.BlockSpec((tm, D_MODEL), lambda i: (i, OFF_GA // D_MODEL)),
            pl.BlockSpec((tm, D_MODEL), lambda i: (i, OFF_GB // D_MODEL)),
            pl.BlockSpec((gvw, D_MODEL), lambda i: (0, 0), **const),
            pl.BlockSpec((qw, D_MODEL), lambda i: (0, 0), **const),
        ],
        out_specs=pl.BlockSpec((tm, D_MODEL), lambda i: (i, 0)),
        scratch_shapes=[pltpu.VMEM((tm, qw), bf16)],
        compiler_params=_cp(("parallel",)),
        name="merge",
    )(y_gla, o_cmp, o_slc, o_win, proj, proj, proj, wg, wn)


def _outp_kernel(m_ref, x_ref, wo_ref, g2_ref, wq_ref, x1_ref, h2_ref, qp_ref):
    x1 = x_ref[...] + _dot(m_ref[...], wo_ref[...])
    x1_ref[...] = x1
    ms = jnp.mean(x1 * x1, axis=-1, keepdims=True)
    h2 = x1 * lax.rsqrt(ms + EPS) * g2_ref[...]
    h2_ref[...] = h2
    qp_ref[...] = _dot(h2.astype(bf16), wq_ref[...])


def _outp(merged, x2, wo, g2, wq, tm):
    T = x2.shape[0]
    pw = PEER_HEADS * PEER_DKEY
    const = dict(pipeline_mode=pl.Buffered(1))
    return pl.pallas_call(
        _outp_kernel,
        out_shape=(jax.ShapeDtypeStruct((T, D_MODEL), f32), jax.ShapeDtypeStruct((T, D_MODEL), f32),
                   jax.ShapeDtypeStruct((T, pw), f32)),
        grid=(T // tm,),
        in_specs=[
            pl.BlockSpec((tm, D_MODEL), lambda i: (i, 0)),
            pl.BlockSpec((tm, D_MODEL), lambda i: (i, 0)),
            pl.BlockSpec((D_MODEL, D_MODEL), lambda i: (0, 0), **const),
            pl.BlockSpec((1, D_MODEL), lambda i: (0, 0)),
            pl.BlockSpec((D_MODEL, pw), lambda i: (0, 0), **const),
        ],
        out_specs=(pl.BlockSpec((tm, D_MODEL), lambda i: (i, 0)), pl.BlockSpec((tm, D_MODEL), lambda i: (i, 0)),
                   pl.BlockSpec((tm, pw), lambda i: (i, 0))),
        compiler_params=_cp(("parallel",)),
        name="outp",
    )(merged, x2, wo, g2, wq)


def _topk_rows(cur, iota0, n, k, payload=None):
    vals, outs = [], []
    for _ in range(k):
        m = jnp.max(cur, axis=0, keepdims=True)
        first = jnp.min(jnp.where(cur == m, iota0, n), axis=0, keepdims=True)
        pick = iota0 == first
        vals.append(m)
        outs.append(first if payload is None else jnp.sum(jnp.where(pick, payload, 0), axis=0, keepdims=True))
        cur = jnp.where(pick, -jnp.inf, cur)
    return jnp.concatenate(vals, axis=0), jnp.concatenate(outs, axis=0)


_PEER_CELLS = [(a, b) for a in range(PEER_TOPK) for b in range(PEER_TOPK // (a + 1))]
_PEER_NCELL = -(-len(_PEER_CELLS) // 8) * 8


def _ptopk_kernel(qp_ref, k1_ref, k2_ref, e_ref, g_ref, *, tt):
    K = PEER_TOPK
    half = PEER_DKEY // 2
    io_keys = lax.broadcasted_iota(i32, (PEER_NKEYS, tt), 0)
    io_cand = lax.broadcasted_iota(i32, (_PEER_NCELL, tt), 0)
    starts = [r for r, (a, b) in enumerate(_PEER_CELLS) if b == 0]
    a_row = jnp.zeros((_PEER_NCELL, tt), i32)
    start_row = jnp.zeros((_PEER_NCELL, tt), i32)
    for st in starts[1:]:
        a_row = a_row + jnp.where(io_cand >= st, 1, 0)
        start_row = jnp.where(io_cand >= st, st, start_row)
    b_row = io_cand - start_row
    real = io_cand < len(_PEER_CELLS)
    for h in range(PEER_HEADS):
        q1 = qp_ref[:, h * PEER_DKEY:h * PEER_DKEY + half]
        q2 = qp_ref[:, h * PEER_DKEY + half:(h + 1) * PEER_DKEY]
        s1 = _nt(k1_ref[h], q1, HI)
        s2 = _nt(k2_ref[h], q2, HI)
        v1, i1 = _topk_rows(s1, io_keys, PEER_NKEYS, K)
        v2, i2 = _topk_rows(s2, io_keys, PEER_NKEYS, K)
        va = jnp.zeros((_PEER_NCELL, tt), f32)
        vb = jnp.zeros((_PEER_NCELL, tt), f32)
        ea = jnp.zeros((_PEER_NCELL, tt), i32)
        eb = jnp.zeros((_PEER_NCELL, tt), i32)
        for r in range(K):
            va = jnp.where(a_row == r, v1[r:r + 1], va)
            ea = jnp.where(a_row == r, i1[r:r + 1], ea)
            vb = jnp.where(b_row == r, v2[r:r + 1], vb)
            eb = jnp.where(b_row == r, i2[r:r + 1], eb)
        cand = jnp.where(real, va + vb, -jnp.inf)
        ecand = ea * PEER_NKEYS + eb
        vals, experts = _topk_rows(cand, io_cand, _PEER_NCELL, K, payload=ecand)
        p = jnp.exp(vals - vals[0:1])
        g_ref[h] = p / jnp.sum(p, axis=0, keepdims=True)
        e_ref[h] = experts


def _ptopk(qp, k1, k2, tt):
    T = qp.shape[0]
    K = PEER_TOPK
    pw = PEER_HEADS * PEER_DKEY
    return pl.pallas_call(
        functools.partial(_ptopk_kernel, tt=tt),
        out_shape=(jax.ShapeDtypeStruct((PEER_HEADS, K, T), i32), jax.ShapeDtypeStruct((PEER_HEADS, K, T), f32)),
        grid=(T // tt,),
        in_specs=[
            pl.BlockSpec((tt, pw), lambda i: (i, 0)),
            pl.BlockSpec((PEER_HEADS, PEER_NKEYS, PEER_DKEY // 2), lambda i: (0, 0, 0)),
            pl.BlockSpec((PEER_HEADS, PEER_NKEYS, PEER_DKEY // 2), lambda i: (0, 0, 0)),
        ],
        out_specs=(pl.BlockSpec((PEER_HEADS, K, tt), lambda i: (0, 0, i)),
                   pl.BlockSpec((PEER_HEADS, K, tt), lambda i: (0, 0, i))),
        compiler_params=_cp(("parallel",)),
        name="ptopk",
    )(qp, k1, k2)


PEER_ROWS = D_MODEL // LANES
PEER_SEL = PEER_HEADS * PEER_TOPK


def _pack_expert_table(peer_u, peer_v):
    ub = lax.bitcast_convert_type(peer_u.astype(bf16), jnp.uint16).astype(jnp.uint32)
    vb = lax.bitcast_convert_type(peer_v.astype(bf16), jnp.uint16).astype(jnp.uint32)
    return (ub | (vb << 16)).reshape(-1, PEER_ROWS, LANES)


def _pexp_kernel(idx_ref, idx_next_ref, g_ref, x_ref, x1_ref, gf_ref, tab_ref, o_ref, buf_a, buf_b, sem, *, tt):
    i = pl.program_id(0)
    n_steps = pl.num_programs(0)

    def start(src_idx_ref, n, dst, dst_sem):
        e = src_idx_ref[0, 0, n]
        pltpu.make_async_copy(tab_ref.at[e], dst.at[:, n, :], dst_sem).start()

    def wait_all(dst, dst_sem):
        pltpu.make_async_copy(dst, dst, dst_sem).wait()

    def compute(src, t):
        rows = slice(t * PEER_SEL, (t + 1) * PEER_SEL)
        acc = jnp.zeros((PEER_SEL, LANES), f32)
        for s in range(PEER_ROWS):
            u = pltpu.bitcast(src[s, rows, :] << 16, f32)
            acc = acc + u * x_ref[0, t:t + 1, s * LANES:(s + 1) * LANES]
        hid = jnp.sum(acc.T, axis=0, keepdims=True)
        w = (g_ref[0, t:t + 1, :] * jax.nn.gelu(hid)).astype(bf16)
        for s in range(PEER_ROWS):
            v = pltpu.bitcast(src[s, rows, :] & jnp.uint32(0xFFFF0000), f32).astype(bf16)
            o_ref[0, t:t + 1, s * LANES:(s + 1) * LANES] = _dot(w, v)

    def step(cur, cur_sem, nxt, nxt_sem):
        wait_all(cur, cur_sem)
        for t in range(tt):
            for k in range(PEER_SEL):
                start(idx_next_ref, t * PEER_SEL + k, nxt, nxt_sem)
            compute(cur, t)
        y = x1_ref[0] + o_ref[0]
        ms = jnp.mean(y * y, axis=-1, keepdims=True)
        o_ref[0] = y * lax.rsqrt(ms + EPS) * gf_ref[...]

    @pl.when(i == 0)
    def _():
        def body(n, carry):
            start(idx_ref, n, buf_a, sem.at[0])
            return carry
        lax.fori_loop(0, tt * PEER_SEL, body, 0, unroll=8)

    @pl.when(lax.rem(i, 2) == 0)
    def _():
        step(buf_a, sem.at[0], buf_b, sem.at[1])

    @pl.when(lax.rem(i, 2) == 1)
    def _():
        step(buf_b, sem.at[1], buf_a, sem.at[0])

    @pl.when((i == n_steps - 1) & (lax.rem(i, 2) == 0))
    def _():
        wait_all(buf_b, sem.at[1])

    @pl.when((i == n_steps - 1) & (lax.rem(i, 2) == 1))
    def _():
        wait_all(buf_a, sem.at[0])


def _pexp(idx3, gates, h2_3d, x1_3d, gf, table, tt):
    n_steps = gates.shape[0]
    rows = pl.BlockSpec((1, tt, D_MODEL), lambda i: (i, 0, 0))
    return pl.pallas_call(
        functools.partial(_pexp_kernel, tt=tt),
        out_shape=jax.ShapeDtypeStruct((n_steps, tt, D_MODEL), f32),
        grid=(n_steps,),
        in_specs=[
            pl.BlockSpec((1, 1, tt * PEER_SEL), lambda i: (i, 0, 0), memory_space=pltpu.SMEM),
            pl.BlockSpec((1, 1, tt * PEER_SEL), lambda i: (jnp.minimum(i + 1, n_steps - 1), 0, 0),
                         memory_space=pltpu.SMEM),
            pl.BlockSpec((1, tt, PEER_SEL), lambda i: (i, 0, 0)),
            rows,
            rows,
            pl.BlockSpec((1, D_MODEL), lambda i: (0, 0)),
            pl.BlockSpec(memory_space=pl.ANY),
        ],
        out_specs=rows,
        scratch_shapes=[pltpu.VMEM((PEER_ROWS, tt * PEER_SEL, LANES), jnp.uint32),
                        pltpu.VMEM((PEER_ROWS, tt * PEER_SEL, LANES), jnp.uint32), pltpu.SemaphoreType.DMA((2,))],
        compiler_params=_cp(("arbitrary",)),
        name="pexp",
    )(idx3, idx3, gates, h2_3d, x1_3d, gf, table)


def _pack_w_in(w):
    zpad = lambda a, n: jnp.pad(a, ((0, 0), (0, n - a.shape[1])))
    return jnp.concatenate(
        [w[:, 0:5120], w[:, 5168:7216], w[:, 7232:12352], zpad(w[:, 5120:5168], LANES), zpad(w[:, 7216:7232], LANES)],
        axis=1).astype(bf16)


def _importance_map(n_cmp_pad, n_slc):
    ratio = SLC_BLOCK // CMP_STRIDE
    nrep = CMP_BLOCK // CMP_STRIDE
    at = np.zeros((n_slc, n_cmp_pad), np.float32)
    for j in range(n_slc):
        for m in range(ratio):
            for n in range(nrep):
                c = ratio * j + m - n
                if 0 <= c < n_cmp_pad:
                    at[j, c] += 1.0
    return jnp.asarray(at)


def _mixers(x2, positions, norm1_g, w_in, gla_w_decay, gla_b_decay, gla_norm_g, pos_k, pos_v, k_w1, k_w2, v_w1, v_w2,
            B, S):
    T = B * S
    G = NSA_GROUPS
    proj = _proj(x2, norm1_g.reshape(1, D_MODEL), _pack_w_in(w_in), tm=min(1024, T), tn=896)

    wd_pad = jnp.pad(gla_w_decay, ((0, LANES - GLA_LOWRANK), (0, 0)))
    y_gla = _gla(proj, wd_pad, gla_b_decay.reshape(1, -1), gla_norm_g.reshape(1, GLA_DV), B, S)

    inv = jnp.power(ROPE_THETA, -(jnp.arange(0, ROPE_DIM, 2, dtype=f32) / ROPE_DIM))
    inv_lane = jnp.concatenate([inv, inv, jnp.zeros((LANES - ROPE_DIM,), f32)]).reshape(1, LANES)
    q_rot, k_rot = _rope(proj, positions.reshape(T, 1), inv_lane, tm=min(512, T))

    n_rows = S // CMP_STRIDE
    kv_w = G * NSA_DH

    def stride_rows(a):
        return a.reshape(B, S, G, NSA_DH).transpose(0, 2, 1, 3).reshape(B * G, n_rows, CMP_STRIDE * NSA_DH)

    r_k = stride_rows(k_rot[:, 0:kv_w])
    r_v = stride_rows(proj[:, OFF_VCM:OFF_VCM + kv_w])
    pos2 = lambda p: p.reshape(2, (CMP_BLOCK // 2) * NSA_DH)
    kc = _compress(r_k, pos2(pos_k), k_w1.astype(bf16), k_w2.astype(bf16))
    vc = _compress(r_v, pos2(pos_v), v_w1.astype(bf16), v_w2.astype(bf16))

    n_slc = S // SLC_BLOCK
    o_cmp, sel_t = _cmp_sel(q_rot, kc, vc, _importance_map(n_rows, n_slc), B, S, tq=min(256, S))
    sel_tm = sel_t.transpose(0, 2, 1)
    o_slc = _slc(q_rot, k_rot, proj, sel_tm, B, S, tq=256, tk=min(512, S))
    o_win = _win(q_rot, k_rot, proj, B, S, tq=256)
    return proj, y_gla, o_cmp, o_slc, o_win


def _peer_stage(x1, h2, qp, sub_k1, sub_k2, peer_u, peer_v, final_g):
    T = x1.shape[0]
    e_t, g_t = _ptopk(qp, sub_k1, sub_k2, tt=256)
    tt = 8
    idx3 = e_t.transpose(2, 0, 1).reshape(T // tt, 1, tt * PEER_SEL)
    gates = g_t.transpose(2, 0, 1).reshape(T // tt, tt, PEER_SEL)
    table = _pack_expert_table(peer_u, peer_v)
    blocks = lambda a: a.reshape(T // tt, tt, D_MODEL)
    out = _pexp(idx3, gates, blocks(h2), blocks(x1), final_g.reshape(1, D_MODEL), table, tt)
    return out.reshape(T, D_MODEL)


def kernel(x, positions, norm1_g, w_in, gla_w_decay, gla_b_decay, gla_norm_g, nsa_cmp_pos_k, nsa_cmp_pos_v,
           nsa_cmp_k_w1, nsa_cmp_k_w2, nsa_cmp_v_w1, nsa_cmp_v_w2, w_up_gla, w_up_nsa, w_out, norm2_g,
           peer_w_q, peer_sub_k1, peer_sub_k2, peer_u, peer_v, final_g):
    B, S, D = x.shape
    T = B * S
    assert norm1_g.shape[0] == 1 and D == D_MODEL
    x2 = x.reshape(T, D)
    proj, y_gla, o_cmp, o_slc, o_win = _mixers(
        x2, positions, norm1_g[0], w_in[0], gla_w_decay[0], gla_b_decay[0], gla_norm_g[0], nsa_cmp_pos_k[0],
        nsa_cmp_pos_v[0], nsa_cmp_k_w1[0], nsa_cmp_k_w2[0], nsa_cmp_v_w1[0], nsa_cmp_v_w2[0], B, S)
    merged = _merge(y_gla, o_cmp, o_slc, o_win, proj, w_up_gla[0].astype(bf16), w_up_nsa[0].astype(bf16), tm=256)
    x1, h2, qp = _outp(merged, x2, w_out[0].astype(bf16), norm2_g[0].reshape(1, D), peer_w_q[0].astype(bf16), tm=256)
    out = _peer_stage(x1, h2, qp, peer_sub_k1[0], peer_sub_k2[0], peer_u[0], peer_v[0], final_g)
    return out.reshape(B, S, D)
```

```python
import functools

import jax
import jax.numpy as jnp
import numpy as np
from jax import lax
from jax.experimental import pallas as pl
from jax.experimental.pallas import tpu as pltpu

f32 = jnp.float32
bf16 = jnp.bfloat16
i32 = jnp.int32

D_MODEL = 2048
EPS = 1e-6
ROPE_THETA = 500000.0
GLA_HEADS = 4
GLA_DK = 128
GLA_DV = 256
GLA_LOWRANK = 16
GLA_TAU = 16.0
GLA_CHUNK = 64
GLA_SUB = 8
NSA_HEADS = 16
NSA_GROUPS = 4
NSA_HPG = 4
NSA_DH = 128
ROPE_DIM = 32
CMP_BLOCK = 32
CMP_STRIDE = 16
SLC_BLOCK = 64
SLC_TOPN = 16
WINDOW = 512
PEER_HEADS = 8
PEER_NKEYS = 128
PEER_DKEY = 128
PEER_TOPK = 16

LANES = 128
OFF_NQ = 0
OFF_KCM = 2048
OFF_VCM = 2560
OFF_KSL = 3072
OFF_VSL = 3584
OFF_KWN = 4096
OFF_VWN = 4608
OFF_GQ = 5120
OFF_GK = 5632
OFF_GV = 6144
OFF_GR = 7168
OFF_GA = 8192
OFF_GB = 10240
OFF_NGATE = 12288
OFF_GLR = 12416
NCOLS = 12544

VMEM_LIMIT = 56 * 1024 * 1024
NEG = -1e30
HI = lax.Precision.HIGHEST


def _cp(sem, vmem=VMEM_LIMIT):
    return pltpu.CompilerParams(dimension_semantics=sem, vmem_limit_bytes=vmem)


def _nt(a, b, precision=None):
    return lax.dot_general(a, b, (((1,), (1,)), ((), ())), preferred_element_type=f32, precision=precision)


def _tn(a, b, precision=None):
    return lax.dot_general(a, b, (((0,), (0,)), ((), ())), preferred_element_type=f32, precision=precision)


def _dot(a, b, precision=None):
    return jnp.dot(a, b, preferred_element_type=f32, precision=precision)


def _proj_kernel(x_ref, g_ref, w_ref, o_ref, h_scr):
    @pl.when(pl.program_id(1) == 0)
    def _():
        x = x_ref[...]
        ms = jnp.mean(x * x, axis=-1, keepdims=True)
        h_scr[...] = (x * lax.rsqrt(ms + EPS) * g_ref[...]).astype(bf16)

    o_ref[...] = _dot(h_scr[...], w_ref[...])


def _proj(x2, g1, w_pack, tm, tn):
    T = x2.shape[0]
    return pl.pallas_call(
        _proj_kernel,
        out_shape=jax.ShapeDtypeStruct((T, NCOLS), f32),
        grid=(T // tm, NCOLS // tn),
        in_specs=[
            pl.BlockSpec((tm, D_MODEL), lambda i, j: (i, 0)),
            pl.BlockSpec((1, D_MODEL), lambda i, j: (0, 0)),
            pl.BlockSpec((D_MODEL, tn), lambda i, j: (0, j)),
        ],
        out_specs=pl.BlockSpec((tm, tn), lambda i, j: (i, j)),
        scratch_shapes=[pltpu.VMEM((tm, D_MODEL), bf16)],
        compiler_params=_cp(("parallel", "arbitrary")),
        name="proj",
    )(x2, g1, w_pack)


def _gla_kernel(gq_ref, gk_ref, gv_ref, gr_ref, glr_ref, wd_ref, bd_ref, ng_ref, tri_ref, o_ref, st_ref):
    C = GLA_CHUNK
    R = GLA_SUB

    @pl.when(pl.program_id(1) == 0)
    def _():
        st_ref[...] = jnp.zeros_like(st_ref)

    glr = glr_ref[...]
    tri = tri_ref[...]
    lane = lax.broadcasted_iota(i32, (R, C), 1)
    row = lax.broadcasted_iota(i32, (R, C), 0)
    for h in range(GLA_HEADS):
        ks = slice(h * GLA_DK, (h + 1) * GLA_DK)
        vs = slice(h * GLA_DV, (h + 1) * GLA_DV)
        z = _dot(glr, wd_ref[:, ks], HI) + bd_ref[:, ks]
        g = (jnp.minimum(z, 0.0) - jnp.log(1.0 + jnp.exp(-jnp.abs(z)))) / GLA_TAU
        b = _dot(tri, g, HI)
        q = gq_ref[:, ks] * (GLA_DK ** -0.5)
        k = gk_ref[:, ks]
        v = gv_ref[:, vs].astype(bf16)
        st = st_ref[h]
        o = _nt((q * jnp.exp(b)).astype(bf16), st.astype(bf16))
        blocks = []
        for i in range(C // R):
            r0 = i * R
            bi = b[r0:r0 + R]
            ref = b[r0:r0 + 1]
            qi = q[r0:r0 + R]
            qt = qi * jnp.exp(bi - ref)
            kt = k * jnp.exp(jnp.minimum(ref - b, 0.0))
            a = _nt(qt.astype(bf16), kt.astype(bf16))
            a = jnp.where(lane < r0, a, 0.0)
            for s in range(R):
                w = qi * k[r0 + s:r0 + s + 1] * jnp.exp(jnp.minimum(bi - b[r0 + s:r0 + s + 1], 0.0))
                col = jnp.sum(w, axis=-1, keepdims=True)
                a = jnp.where(lane == r0 + s, jnp.where(row >= s, col, 0.0), a)
            blocks.append(a)
        attn = jnp.concatenate(blocks, axis=0)
        o = o + _dot(attn.astype(bf16), v)
        bl = b[C - 1:C]
        kd = k * jnp.exp(bl - b)
        st_ref[h] = jnp.exp(bl) * st + _tn(v, kd.astype(bf16))
        ms = jnp.mean(o * o, axis=-1, keepdims=True)
        y = o * lax.rsqrt(ms + EPS) * ng_ref[...]
        r = gr_ref[:, vs]
        o_ref[:, vs] = y * (r * jax.nn.sigmoid(r))


def _gla(proj, wd_pad, bd, ng, B, S):
    C = GLA_CHUNK
    nc = S // C
    tri = jnp.asarray(np.tril(np.ones((C, C), np.float32)))
    qk_w = GLA_HEADS * GLA_DK
    v_w = GLA_HEADS * GLA_DV
    row = lambda b, c: b * nc + c
    return pl.pallas_call(
        _gla_kernel,
        out_shape=jax.ShapeDtypeStruct((B * S, v_w), f32),
        grid=(B, nc),
        in_specs=[
            pl.BlockSpec((C, qk_w), lambda b, c: (row(b, c), OFF_GQ // qk_w)),
            pl.BlockSpec((C, qk_w), lambda b, c: (row(b, c), OFF_GK // qk_w)),
            pl.BlockSpec((C, v_w), lambda b, c: (row(b, c), OFF_GV // v_w)),
            pl.BlockSpec((C, v_w), lambda b, c: (row(b, c), OFF_GR // v_w)),
            pl.BlockSpec((C, LANES), lambda b, c: (row(b, c), OFF_GLR // LANES)),
            pl.BlockSpec((LANES, qk_w), lambda b, c: (0, 0)),
            pl.BlockSpec((1, qk_w), lambda b, c: (0, 0)),
            pl.BlockSpec((1, GLA_DV), lambda b, c: (0, 0)),
            pl.BlockSpec((C, C), lambda b, c: (0, 0)),
        ],
        out_specs=pl.BlockSpec((C, v_w), lambda b, c: (row(b, c), 0)),
        scratch_shapes=[pltpu.VMEM((GLA_HEADS, GLA_DV, GLA_DK), f32)],
        compiler_params=_cp(("parallel", "arbitrary")),
        name="gla",
    )(proj, proj, proj, proj, proj, wd_pad, bd, ng, tri)


def _rope_kernel(q_ref, kc_ref, ks_ref, kw_ref, pos_ref, inv_ref, qo_ref, ko_ref):
    ang = pos_ref[...].astype(f32) * inv_ref[...]
    cos = jnp.cos(ang)
    sin = jnp.sin(ang)
    lane = lax.broadcasted_iota(i32, ang.shape, 1)
    half = ROPE_DIM // 2
    s_up = jnp.where((lane >= half) & (lane < ROPE_DIM), sin, 0.0)
    s_dn = jnp.where(lane < half, -sin, 0.0)

    def rot(x):
        return x * cos + pltpu.roll(x, half, 1) * s_up + pltpu.roll(x, LANES - half, 1) * s_dn

    for h in range(NSA_HEADS):
        sl = slice(h * NSA_DH, (h + 1) * NSA_DH)
        qo_ref[:, sl] = rot(q_ref[:, sl])
    kv_w = NSA_GROUPS * NSA_DH
    for n, ref in enumerate((kc_ref, ks_ref, kw_ref)):
        for g in range(NSA_GROUPS):
            sl = slice(g * NSA_DH, (g + 1) * NSA_DH)
            ko_ref[:, n * kv_w + g * NSA_DH:n * kv_w + (g + 1) * NSA_DH] = rot(ref[:, sl])


def _rope(proj, pos_col, inv_lane, tm):
    T = proj.shape[0]
    q_w = NSA_HEADS * NSA_DH
    kv_w = NSA_GROUPS * NSA_DH
    return pl.pallas_call(
        _rope_kernel,
        out_shape=(jax.ShapeDtypeStruct((T, q_w), f32), jax.ShapeDtypeStruct((T, 3 * kv_w), f32)),
        grid=(T // tm,),
        in_specs=[
            pl.BlockSpec((tm, q_w), lambda i: (i, OFF_NQ // q_w)),
            pl.BlockSpec((tm, kv_w), lambda i: (i, OFF_KCM // kv_w)),
            pl.BlockSpec((tm, kv_w), lambda i: (i, OFF_KSL // kv_w)),
            pl.BlockSpec((tm, kv_w), lambda i: (i, OFF_KWN // kv_w)),
            pl.BlockSpec((tm, 1), lambda i: (i, 0)),
            pl.BlockSpec((1, LANES), lambda i: (0, 0)),
        ],
        out_specs=(pl.BlockSpec((tm, q_w), lambda i: (i, 0)), pl.BlockSpec((tm, 3 * kv_w), lambda i: (i, 0))),
        compiler_params=_cp(("parallel",)),
        name="rope",
    )(proj, proj, proj, proj, pos_col, inv_lane)


def _compress_kernel(r_ref, pos_ref, w1_ref, w2_ref, o_ref):
    half = (CMP_BLOCK // 2) * NSA_DH
    r = r_ref[0]
    n_rows = r.shape[0]
    ra = (r + pos_ref[0:1, :]).astype(bf16)
    rb = (pltpu.roll(r, n_rows - 1, 0) + pos_ref[1:2, :]).astype(bf16)
    hid = _dot(ra, w1_ref[0:half, :]) + _dot(rb, w1_ref[half:2 * half, :])
    o_ref[0] = _dot(jax.nn.gelu(hid).astype(bf16), w2_ref[...])


def _compress(r, pos2, w1, w2):
    BG, n_rows, width = r.shape
    return pl.pallas_call(
        _compress_kernel,
        out_shape=jax.ShapeDtypeStruct((BG, n_rows, NSA_DH), f32),
        grid=(BG,),
        in_specs=[
            pl.BlockSpec((1, n_rows, width), lambda i: (i, 0, 0)),
            pl.BlockSpec((2, width), lambda i: (0, 0)),
            pl.BlockSpec((2 * width, NSA_DH), lambda i: (0, 0)),
            pl.BlockSpec((NSA_DH, NSA_DH), lambda i: (0, 0)),
        ],
        out_specs=pl.BlockSpec((1, n_rows, NSA_DH), lambda i: (i, 0, 0)),
        compiler_params=_cp(("parallel",)),
        name="compress",
    )(r, pos2, w1, w2)


def _cmp_sel_kernel(q_ref, kc_ref, vc_ref, at_ref, o_ref, sel_ref, *, tq, n_slc):
    i = pl.program_id(2)
    ncp = kc_ref.shape[1]
    scale = NSA_DH ** -0.5
    t = i * tq + lax.broadcasted_iota(i32, (tq, ncp), 0)
    n_end = lax.broadcasted_iota(i32, (tq, ncp), 1) * CMP_STRIDE + (CMP_BLOCK - 1)
    cmask = n_end <= t
    kc = kc_ref[0].astype(bf16)
    vc = vc_ref[0].astype(bf16)
    pg = jnp.zeros((tq, ncp), f32)
    for hp in range(NSA_HPG):
        sl = slice(hp * NSA_DH, (hp + 1) * NSA_DH)
        s = _nt(q_ref[:, sl].astype(bf16), kc) * scale
        s = jnp.where(cmask, s, NEG)
        m = jnp.max(s, axis=-1, keepdims=True)
        p = jnp.where(cmask, jnp.exp(s - m), 0.0)
        p = p / jnp.maximum(jnp.sum(p, axis=-1, keepdims=True), 1e-30)
        o_ref[:, sl] = _dot(p.astype(bf16), vc)
        pg = pg + p
    imp = _nt(at_ref[...], pg, HI)
    j = lax.broadcasted_iota(i32, (n_slc, tq), 0)
    blk = (i * tq + lax.broadcasted_iota(i32, (n_slc, tq), 1)) // SLC_BLOCK
    forced = (j == 0) | (j == blk) | (j == blk - 1)
    imp = jnp.where(forced, 1e9, jnp.where(j > blk, -1e9, imp))
    sel = jnp.zeros((n_slc, tq), f32)
    cur = imp
    for _ in range(min(SLC_TOPN, n_slc)):
        m = jnp.max(cur, axis=0, keepdims=True)
        first = jnp.min(jnp.where(cur == m, j, n_slc), axis=0, keepdims=True)
        pick = j == first
        sel = jnp.where(pick, 1.0, sel)
        cur = jnp.where(pick, -jnp.inf, cur)
    sel_ref[0] = sel


def _cmp_sel(q_rot, kc, vc, at, B, S, tq):
    nq = S // tq
    G = NSA_GROUPS
    ncp = kc.shape[1]
    n_slc = S // SLC_BLOCK
    gw = NSA_HPG * NSA_DH
    return pl.pallas_call(
        functools.partial(_cmp_sel_kernel, tq=tq, n_slc=n_slc),
        out_shape=(jax.ShapeDtypeStruct((B * S, NSA_HEADS * NSA_DH), f32),
                   jax.ShapeDtypeStruct((B * G, n_slc, S), f32)),
        grid=(B, G, nq),
        in_specs=[
            pl.BlockSpec((tq, gw), lambda b, g, i: (b * nq + i, g)),
            pl.BlockSpec((1, ncp, NSA_DH), lambda b, g, i: (b * G + g, 0, 0)),
            pl.BlockSpec((1, ncp, NSA_DH), lambda b, g, i: (b * G + g, 0, 0)),
            pl.BlockSpec((n_slc, ncp), lambda b, g, i: (0, 0)),
        ],
        out_specs=(pl.BlockSpec((tq, gw), lambda b, g, i: (b * nq + i, g)),
                   pl.BlockSpec((1, n_slc, tq), lambda b, g, i: (b * G + g, 0, i))),
        compiler_params=_cp(("parallel", "parallel", "parallel")),
        name="cmp_sel",
    )(q_rot, kc, vc, at)


def _slc_kernel(q_ref, k_ref, v_ref, sel_ref, o_ref, m_scr, l_scr, acc_scr, *, tq, tk, n_slc):
    i = pl.program_id(2)
    j = pl.program_id(3)
    nk = pl.num_programs(3)
    scale = NSA_DH ** -0.5

    @pl.when(j == 0)
    def _():
        m_scr[...] = jnp.full_like(m_scr, NEG)
        l_scr[...] = jnp.zeros_like(l_scr)
        acc_scr[...] = jnp.zeros_like(acc_scr)

    @pl.when(j * tk < (i + 1) * tq)
    def _():
        kblk = (j * tk + lax.broadcasted_iota(i32, (n_slc, tk), 1)) // SLC_BLOCK
        expand = jnp.where(kblk == lax.broadcasted_iota(i32, (n_slc, tk), 0), 1.0, 0.0).astype(bf16)
        picked = _dot(sel_ref[0].astype(bf16), expand)
        t = i * tq + lax.broadcasted_iota(i32, (tq, tk), 0)
        kpos = j * tk + lax.broadcasted_iota(i32, (tq, tk), 1)
        mask = jnp.where(kpos <= t, picked, 0.0) > 0.5
        k = k_ref[...].astype(bf16)
        v = v_ref[...].astype(bf16)
        for hp in range(NSA_HPG):
            sl = slice(hp * NSA_DH, (hp + 1) * NSA_DH)
            s = _nt(q_ref[:, sl].astype(bf16), k) * scale
            s = jnp.where(mask, s, NEG)
            m_old = m_scr[hp]
            m_new = jnp.maximum(m_old, jnp.max(s, axis=-1, keepdims=True))
            alpha = jnp.exp(m_old - m_new)
            p = jnp.where(mask, jnp.exp(s - m_new), 0.0)
            l_scr[hp] = alpha * l_scr[hp] + jnp.sum(p, axis=-1, keepdims=True)
            acc_scr[hp] = alpha * acc_scr[hp] + _dot(p.astype(bf16), v)
            m_scr[hp] = m_new

    @pl.when(j == nk - 1)
    def _():
        for hp in range(NSA_HPG):
            sl = slice(hp * NSA_DH, (hp + 1) * NSA_DH)
            o_ref[:, sl] = acc_scr[hp] / jnp.maximum(l_scr[hp], 1e-30)


def _slc(q_rot, k_rot, proj, sel_tm, B, S, tq, tk):
    nq, nk = S // tq, S // tk
    G = NSA_GROUPS
    n_slc = S // SLC_BLOCK
    gw = NSA_HPG * NSA_DH
    kv_w = NSA_GROUPS * NSA_DH

    def kj(i, j):
        return jnp.minimum(j, ((i + 1) * tq - 1) // tk)

    return pl.pallas_call(
        functools.partial(_slc_kernel, tq=tq, tk=tk, n_slc=n_slc),
        out_shape=jax.ShapeDtypeStruct((B * S, NSA_HEADS * NSA_DH), f32),
        grid=(B, G, nq, nk),
        in_specs=[
            pl.BlockSpec((tq, gw), lambda b, g, i, j: (b * nq + i, g)),
            pl.BlockSpec((tk, NSA_DH), lambda b, g, i, j: (b * nk + kj(i, j), kv_w // NSA_DH + g)),
            pl.BlockSpec((tk, NSA_DH), lambda b, g, i, j: (b * nk + kj(i, j), OFF_VSL // NSA_DH + g)),
            pl.BlockSpec((1, tq, n_slc), lambda b, g, i, j: (b * G + g, i, 0)),
        ],
        out_specs=pl.BlockSpec((tq, gw), lambda b, g, i, j: (b * nq + i, g)),
        scratch_shapes=[pltpu.VMEM((NSA_HPG, tq, 1), f32), pltpu.VMEM((NSA_HPG, tq, 1), f32),
                        pltpu.VMEM((NSA_HPG, tq, NSA_DH), f32)],
        compiler_params=_cp(("parallel", "parallel", "parallel", "arbitrary")),
        name="slc",
    )(q_rot, k_rot, proj, sel_tm)


def _win_kernel(q_ref, k0_ref, k1_ref, k2_ref, v0_ref, v1_ref, v2_ref, o_ref, *, tq):
    i = pl.program_id(2)
    scale = NSA_DH ** -0.5
    nb = WINDOW // tq + 1
    k = jnp.concatenate([k0_ref[...], k1_ref[...], k2_ref[...]], axis=0).astype(bf16)
    v = jnp.concatenate([v0_ref[...], v1_ref[...], v2_ref[...]], axis=0).astype(bf16)
    t = i * tq + lax.broadcasted_iota(i32, (tq, nb * tq), 0)
    kpos = (i - (nb - 1)) * tq + lax.broadcasted_iota(i32, (tq, nb * tq), 1)
    mask = jnp.where(kpos >= 0, jnp.where(kpos <= t, jnp.where(kpos > t - WINDOW, 1.0, 0.0), 0.0), 0.0) > 0.5
    for hp in range(NSA_HPG):
        sl = slice(hp * NSA_DH, (hp + 1) * NSA_DH)
        s = _nt(q_ref[:, sl].astype(bf16), k) * scale
        s = jnp.where(mask, s, NEG)
        m = jnp.max(s, axis=-1, keepdims=True)
        p = jnp.where(mask, jnp.exp(s - m), 0.0)
        p = p / jnp.maximum(jnp.sum(p, axis=-1, keepdims=True), 1e-30)
        o_ref[:, sl] = _dot(p.astype(bf16), v)


def _win(q_rot, k_rot, proj, B, S, tq):
    assert WINDOW % tq == 0 and WINDOW // tq == 2
    nq = S // tq
    gw = NSA_HPG * NSA_DH
    kv_w = NSA_GROUPS * NSA_DH

    def kspec(d, col0):
        return pl.BlockSpec((tq, NSA_DH), lambda b, g, i: (b * nq + jnp.maximum(i - 2 + d, 0), col0 + g))

    return pl.pallas_call(
        functools.partial(_win_kernel, tq=tq),
        out_shape=jax.ShapeDtypeStruct((B * S, NSA_HEADS * NSA_DH), f32),
        grid=(B, NSA_GROUPS, nq),
        in_specs=[pl.BlockSpec((tq, gw), lambda b, g, i: (b * nq + i, g))]
        + [kspec(d, 2 * kv_w // NSA_DH) for d in range(3)]
        + [kspec(d, OFF_VWN // NSA_DH) for d in range(3)],
        out_specs=pl.BlockSpec((tq, gw), lambda b, g, i: (b * nq + i, g)),
        compiler_params=_cp(("parallel", "parallel", "parallel")),
        name="win",
    )(q_rot, k_rot, k_rot, k_rot, proj, proj, proj)


def _merge_kernel(yg_ref, oc_ref, os_ref, ow_ref, ng_ref, ga_ref, gb_ref, wg_ref, wn_ref, o_ref, yn_scr):
    sg = jax.nn.sigmoid(ng_ref[...])
    for h in range(NSA_HEADS):
        sl = slice(h * NSA_DH, (h + 1) * NSA_DH)
        y = (sg[:, 3 * h:3 * h + 1] * oc_ref[:, sl] + sg[:, 3 * h + 1:3 * h + 2] * os_ref[:, sl]
             + sg[:, 3 * h + 2:3 * h + 3] * ow_ref[:, sl])
        yn_scr[:, sl] = y.astype(bf16)
    up_g = _dot(yg_ref[...].astype(bf16), wg_ref[...])
    up_n = _dot(yn_scr[...], wn_ref[...])
    o_ref[...] = (jax.nn.sigmoid(ga_ref[...]) * up_g + jax.nn.sigmoid(gb_ref[...]) * up_n).astype(o_ref.dtype)


def _merge(y_gla, o_cmp, o_slc, o_win, proj, wg, wn, tm):
    T = y_gla.shape[0]
    qw = NSA_HEADS * NSA_DH
    gvw = GLA_HEADS * GLA_DV
    const = dict(pipeline_mode=pl.Buffered(1))
    return pl.pallas_call(
        _merge_kernel,
        out_shape=jax.ShapeDtypeStruct((T, D_MODEL), bf16),
        grid=(T // tm,),
        in_specs=[
            pl.BlockSpec((tm, gvw), lambda i: (i, 0)),
            pl.BlockSpec((tm, qw), lambda i: (i, 0)),
            pl.BlockSpec((tm, qw), lambda i: (i, 0)),
            pl.BlockSpec((tm, qw), lambda i: (i, 0)),
            pl.BlockSpec((tm, LANES), lambda i: (i, OFF_NGATE // LANES)),
            pl.BlockSpec((tm, D_MODEL), lambda i: (i, OFF_GA // D_MODEL)),
            pl.BlockSpec((tm, D_MODEL), lambda i: (i, OFF_GB // D_MODEL)),
            pl.BlockSpec((gvw, D_MODEL), lambda i: (0, 0), **const),
            pl.BlockSpec((qw, D_MODEL), lambda i: (0, 0), **const),
        ],
        out_specs=pl.BlockSpec((tm, D_MODEL), lambda i: (i, 0)),
        scratch_shapes=[pltpu.VMEM((tm, qw), bf16)],
        compiler_params=_cp(("parallel",)),
        name="merge",
    )(y_gla, o_cmp, o_slc, o_win, proj, proj, proj, wg, wn)


def _outp_kernel(m_ref, x_ref, wo_ref, g2_ref, wq_ref, x1_ref, h2_ref, qp_ref):
    x1 = x_ref[...] + _dot(m_ref[...], wo_ref[...])
    x1_ref[...] = x1
    ms = jnp.mean(x1 * x1, axis=-1, keepdims=True)
    h2 = x1 * lax.rsqrt(ms + EPS) * g2_ref[...]
    h2_ref[...] = h2
    qp_ref[...] = _dot(h2.astype(bf16), wq_ref[...])


def _outp(merged, x2, wo, g2, wq, tm):
    T = x2.shape[0]
    pw = PEER_HEADS * PEER_DKEY
    const = dict(pipeline_mode=pl.Buffered(1))
    return pl.pallas_call(
        _outp_kernel,
        out_shape=(jax.ShapeDtypeStruct((T, D_MODEL), f32), jax.ShapeDtypeStruct((T, D_MODEL), f32),
                   jax.ShapeDtypeStruct((T, pw), f32)),
        grid=(T // tm,),
        in_specs=[
            pl.BlockSpec((tm, D_MODEL), lambda i: (i, 0)),
            pl.BlockSpec((tm, D_MODEL), lambda i: (i, 0)),
            pl.BlockSpec((D_MODEL, D_MODEL), lambda i: (0, 0), **const),
            pl.BlockSpec((1, D_MODEL), lambda i: (0, 0)),
            pl.BlockSpec((D_MODEL, pw), lambda i: (0, 0), **const),
        ],
        out_specs=(pl.BlockSpec((tm, D_MODEL), lambda i: (i, 0)), pl.BlockSpec((tm, D_MODEL), lambda i: (i, 0)),
                   pl.BlockSpec((tm, pw), lambda i: (i, 0))),
        compiler_params=_cp(("parallel",)),
        name="outp",
    )(merged, x2, wo, g2, wq)


def _topk_rows(cur, iota0, n, k, payload=None):
    vals, outs = [], []
    for _ in range(k):
        m = jnp.max(cur, axis=0, keepdims=True)
        first = jnp.min(jnp.where(cur == m, iota0, n), axis=0, keepdims=True)
        pick = iota0 == first
        vals.append(m)
        outs.append(first if payload is None else jnp.sum(jnp.where(pick, payload, 0), axis=0, keepdims=True))
        cur = jnp.where(pick, -jnp.inf, cur)
    return jnp.concatenate(vals, axis=0), jnp.concatenate(outs, axis=0)


_PEER_CELLS = [(a, b) for a in range(PEER_TOPK) for b in range(PEER_TOPK // (a + 1))]
_PEER_NCELL = -(-len(_PEER_CELLS) // 8) * 8


def _ptopk_kernel(qp_ref, k1_ref, k2_ref, e_ref, g_ref, *, tt):
    K = PEER_TOPK
    half = PEER_DKEY // 2
    io_keys = lax.broadcasted_iota(i32, (PEER_NKEYS, tt), 0)
    io_cand = lax.broadcasted_iota(i32, (_PEER_NCELL, tt), 0)
    starts = [r for r, (a, b) in enumerate(_PEER_CELLS) if b == 0]
    a_row = jnp.zeros((_PEER_NCELL, tt), i32)
    start_row = jnp.zeros((_PEER_NCELL, tt), i32)
    for st in starts[1:]:
        a_row = a_row + jnp.where(io_cand >= st, 1, 0)
        start_row = jnp.where(io_cand >= st, st, start_row)
    b_row = io_cand - start_row
    real = io_cand < len(_PEER_CELLS)
    for h in range(PEER_HEADS):
        q1 = qp_ref[:, h * PEER_DKEY:h * PEER_DKEY + half]
        q2 = qp_ref[:, h * PEER_DKEY + half:(h + 1) * PEER_DKEY]
        s1 = _nt(k1_ref[h], q1, HI)
        s2 = _nt(k2_ref[h], q2, HI)
        v1, i1 = _topk_rows(s1, io_keys, PEER_NKEYS, K)
        v2, i2 = _topk_rows(s2, io_keys, PEER_NKEYS, K)
        va = jnp.zeros((_PEER_NCELL, tt), f32)
        vb = jnp.zeros((_PEER_NCELL, tt), f32)
        ea = jnp.zeros((_PEER_NCELL, tt), i32)
        eb = jnp.zeros((_PEER_NCELL, tt), i32)
        for r in range(K):
            va = jnp.where(a_row == r, v1[r:r + 1], va)
            ea = jnp.where(a_row == r, i1[r:r + 1], ea)
            vb = jnp.where(b_row == r, v2[r:r + 1], vb)
            eb = jnp.where(b_row == r, i2[r:r + 1], eb)
        cand = jnp.where(real, va + vb, -jnp.inf)
        ecand = ea * PEER_NKEYS + eb
        vals, experts = _topk_rows(cand, io_cand, _PEER_NCELL, K, payload=ecand)
        p = jnp.exp(vals - vals[0:1])
        g_ref[h] = p / jnp.sum(p, axis=0, keepdims=True)
        e_ref[h] = experts


def _ptopk(qp, k1, k2, tt):
    T = qp.shape[0]
    K = PEER_TOPK
    pw = PEER_HEADS * PEER_DKEY
    return pl.pallas_call(
        functools.partial(_ptopk_kernel, tt=tt),
        out_shape=(jax.ShapeDtypeStruct((PEER_HEADS, K, T), i32), jax.ShapeDtypeStruct((PEER_HEADS, K, T), f32)),
        grid=(T // tt,),
        in_specs=[
            pl.BlockSpec((tt, pw), lambda i: (i, 0)),
            pl.BlockSpec((PEER_HEADS, PEER_NKEYS, PEER_DKEY // 2), lambda i: (0, 0, 0)),
            pl.BlockSpec((PEER_HEADS, PEER_NKEYS, PEER_DKEY // 2), lambda i: (0, 0, 0)),
        ],
        out_specs=(pl.BlockSpec((PEER_HEADS, K, tt), lambda i: (0, 0, i)),
                   pl.BlockSpec((PEER_HEADS, K, tt), lambda i: (0, 0, i))),
        compiler_params=_cp(("parallel",)),
        name="ptopk",
    )(qp, k1, k2)


PEER_ROWS = D_MODEL // LANES
PEER_SEL = PEER_HEADS * PEER_TOPK


def _pack_expert_table(peer_u, peer_v):
    ub = lax.bitcast_convert_type(peer_u.astype(bf16), jnp.uint16).astype(jnp.uint32)
    vb = lax.bitcast_convert_type(peer_v.astype(bf16), jnp.uint16).astype(jnp.uint32)
    return (ub | (vb << 16)).reshape(-1, PEER_ROWS, LANES)


def _pexp_kernel(idx_ref, idx_next_ref, g_ref, x_ref, x1_ref, gf_ref, tab_ref, o_ref, a0, a1, b0, b1, sem, *, tt):
    i = pl.program_id(0)
    n_steps = pl.num_programs(0)
    th = tt // 2

    def start(src_idx_ref, t, k, halves, sems):
        e = src_idx_ref[0, 0, t * PEER_SEL + k]
        h, tl = divmod(t, th)
        pltpu.make_async_copy(tab_ref.at[e], halves[h].at[:, tl * PEER_SEL + k, :], sems[h]).start()

    def wait_all(dst, dst_sem):
        pltpu.make_async_copy(dst, dst, dst_sem).wait()

    def compute(src, tl, t):
        rows = slice(tl * PEER_SEL, (tl + 1) * PEER_SEL)
        acc = jnp.zeros((PEER_SEL, LANES), f32)
        for s in range(PEER_ROWS):
            u = pltpu.bitcast(src[s, rows, :] << 16, f32)
            acc = acc + u * x_ref[0, t:t + 1, s * LANES:(s + 1) * LANES]
        hid = jnp.sum(acc.T, axis=0, keepdims=True)
        w = (g_ref[0, t:t + 1, :] * jax.nn.gelu(hid)).astype(bf16)
        for s in range(PEER_ROWS):
            v = pltpu.bitcast(src[s, rows, :] & jnp.uint32(0xFFFF0000), f32).astype(bf16)
            o_ref[0, t:t + 1, s * LANES:(s + 1) * LANES] = _dot(w, v)

    def step(cur, cur_sems, nxt, nxt_sems):
        for h in range(2):
            wait_all(cur[h], cur_sems[h])
            for tl in range(th):
                t = h * th + tl
                for k in range(PEER_SEL):
                    start(idx_next_ref, t, k, nxt, nxt_sems)
                compute(cur[h], tl, t)
        y = x1_ref[0] + o_ref[0]
        ms = jnp.mean(y * y, axis=-1, keepdims=True)
        o_ref[0] = y * lax.rsqrt(ms + EPS) * gf_ref[...]

    bufs_a, sems_a = (a0, a1), (sem.at[0], sem.at[1])
    bufs_b, sems_b = (b0, b1), (sem.at[2], sem.at[3])

    @pl.when(i == 0)
    def _():
        for h in range(2):
            def body(n, carry, h=h):
                e = idx_ref[0, 0, h * th * PEER_SEL + n]
                pltpu.make_async_copy(tab_ref.at[e], bufs_a[h].at[:, n, :], sems_a[h]).start()
                return carry
            lax.fori_loop(0, th * PEER_SEL, body, 0, unroll=8)

    @pl.when(lax.rem(i, 2) == 0)
    def _():
        step(bufs_a, sems_a, bufs_b, sems_b)

    @pl.when(lax.rem(i, 2) == 1)
    def _():
        step(bufs_b, sems_b, bufs_a, sems_a)

    @pl.when((i == n_steps - 1) & (lax.rem(i, 2) == 0))
    def _():
        wait_all(b0, sems_b[0])
        wait_all(b1, sems_b[1])

    @pl.when((i == n_steps - 1) & (lax.rem(i, 2) == 1))
    def _():
        wait_all(a0, sems_a[0])
        wait_all(a1, sems_a[1])


def _pexp(idx3, gates, h2_3d, x1_3d, gf, table, tt):
    n_steps = gates.shape[0]
    rows = pl.BlockSpec((1, tt, D_MODEL), lambda i: (i, 0, 0))
    return pl.pallas_call(
        functools.partial(_pexp_kernel, tt=tt),
        out_shape=jax.ShapeDtypeStruct((n_steps, tt, D_MODEL), f32),
        grid=(n_steps,),
        in_specs=[
            pl.BlockSpec((1, 1, tt * PEER_SEL), lambda i: (i, 0, 0), memory_space=pltpu.SMEM),
            pl.BlockSpec((1, 1, tt * PEER_SEL), lambda i: (jnp.minimum(i + 1, n_steps - 1), 0, 0),
                         memory_space=pltpu.SMEM),
            pl.BlockSpec((1, tt, PEER_SEL), lambda i: (i, 0, 0)),
            rows,
            rows,
            pl.BlockSpec((1, D_MODEL), lambda i: (0, 0)),
            pl.BlockSpec(memory_space=pl.ANY),
        ],
        out_specs=rows,
        scratch_shapes=[pltpu.VMEM((PEER_ROWS, (tt // 2) * PEER_SEL, LANES), jnp.uint32) for _ in range(4)]
        + [pltpu.SemaphoreType.DMA((4,))],
        compiler_params=_cp(("arbitrary",)),
        name="pexp",
    )(idx3, idx3, gates, h2_3d, x1_3d, gf, table)


def _pack_w_in(w):
    zpad = lambda a, n: jnp.pad(a, ((0, 0), (0, n - a.shape[1])))
    return jnp.concatenate(
        [w[:, 0:5120], w[:, 5168:7216], w[:, 7232:12352], zpad(w[:, 5120:5168], LANES), zpad(w[:, 7216:7232], LANES)],
        axis=1).astype(bf16)


def _importance_map(n_cmp_pad, n_slc):
    ratio = SLC_BLOCK // CMP_STRIDE
    nrep = CMP_BLOCK // CMP_STRIDE
    at = np.zeros((n_slc, n_cmp_pad), np.float32)
    for j in range(n_slc):
        for m in range(ratio):
            for n in range(nrep):
                c = ratio * j + m - n
                if 0 <= c < n_cmp_pad:
                    at[j, c] += 1.0
    return jnp.asarray(at)


def _mixers(x2, positions, norm1_g, w_in, gla_w_decay, gla_b_decay, gla_norm_g, pos_k, pos_v, k_w1, k_w2, v_w1, v_w2,
            B, S):
    T = B * S
    G = NSA_GROUPS
    proj = _proj(x2, norm1_g.reshape(1, D_MODEL), _pack_w_in(w_in), tm=min(1024, T), tn=896)

    wd_pad = jnp.pad(gla_w_decay, ((0, LANES - GLA_LOWRANK), (0, 0)))
    y_gla = _gla(proj, wd_pad, gla_b_decay.reshape(1, -1), gla_norm_g.reshape(1, GLA_DV), B, S)

    inv = jnp.power(ROPE_THETA, -(jnp.arange(0, ROPE_DIM, 2, dtype=f32) / ROPE_DIM))
    inv_lane = jnp.concatenate([inv, inv, jnp.zeros((LANES - ROPE_DIM,), f32)]).reshape(1, LANES)
    q_rot, k_rot = _rope(proj, positions.reshape(T, 1), inv_lane, tm=min(512, T))

    n_rows = S // CMP_STRIDE
    kv_w = G * NSA_DH

    def stride_rows(a):
        return a.reshape(B, S, G, NSA_DH).transpose(0, 2, 1, 3).reshape(B * G, n_rows, CMP_STRIDE * NSA_DH)

    r_k = stride_rows(k_rot[:, 0:kv_w])
    r_v = stride_rows(proj[:, OFF_VCM:OFF_VCM + kv_w])
    pos2 = lambda p: p.reshape(2, (CMP_BLOCK // 2) * NSA_DH)
    kc = _compress(r_k, pos2(pos_k), k_w1.astype(bf16), k_w2.astype(bf16))
    vc = _compress(r_v, pos2(pos_v), v_w1.astype(bf16), v_w2.astype(bf16))

    n_slc = S // SLC_BLOCK
    o_cmp, sel_t = _cmp_sel(q_rot, kc, vc, _importance_map(n_rows, n_slc), B, S, tq=min(256, S))
    sel_tm = sel_t.transpose(0, 2, 1)
    o_slc = _slc(q_rot, k_rot, proj, sel_tm, B, S, tq=256, tk=min(512, S))
    o_win = _win(q_rot, k_rot, proj, B, S, tq=256)
    return proj, y_gla, o_cmp, o_slc, o_win


def _peer_stage(x1, h2, qp, sub_k1, sub_k2, peer_u, peer_v, final_g):
    T = x1.shape[0]
    e_t, g_t = _ptopk(qp, sub_k1, sub_k2, tt=256)
    tt = 16
    idx3 = e_t.transpose(2, 0, 1).reshape(T // tt, 1, tt * PEER_SEL)
    gates = g_t.transpose(2, 0, 1).reshape(T // tt, tt, PEER_SEL)
    table = _pack_expert_table(peer_u, peer_v)
    blocks = lambda a: a.reshape(T // tt, tt, D_MODEL)
    out = _pexp(idx3, gates, blocks(h2), blocks(x1), final_g.reshape(1, D_MODEL), table, tt)
    return out.reshape(T, D_MODEL)


def kernel(x, positions, norm1_g, w_in, gla_w_decay, gla_b_decay, gla_norm_g, nsa_cmp_pos_k, nsa_cmp_pos_v,
           nsa_cmp_k_w1, nsa_cmp_k_w2, nsa_cmp_v_w1, nsa_cmp_v_w2, w_up_gla, w_up_nsa, w_out, norm2_g,
           peer_w_q, peer_sub_k1, peer_sub_k2, peer_u, peer_v, final_g):
    B, S, D = x.shape
    T = B * S
    assert norm1_g.shape[0] == 1 and D == D_MODEL
    x2 = x.reshape(T, D)
    proj, y_gla, o_cmp, o_slc, o_win = _mixers(
        x2, positions, norm1_g[0], w_in[0], gla_w_decay[0], gla_b_decay[0], gla_norm_g[0], nsa_cmp_pos_k[0],
        nsa_cmp_pos_v[0], nsa_cmp_k_w1[0], nsa_cmp_k_w2[0], nsa_cmp_v_w1[0], nsa_cmp_v_w2[0], B, S)
    merged = _merge(y_gla, o_cmp, o_slc, o_win, proj, w_up_gla[0].astype(bf16), w_up_nsa[0].astype(bf16), tm=256)
    x1, h2, qp = _outp(merged, x2, w_out[0].astype(bf16), norm2_g[0].reshape(1, D), peer_w_q[0].astype(bf16), tm=256)
    out = _peer_stage(x1, h2, qp, peer_sub_k1[0], peer_sub_k2[0], peer_u[0], peer_v[0], final_g)
    return out.reshape(B, S, D)
```

```python
import functools

import jax
import jax.numpy as jnp
import numpy as np
from jax import lax
from jax.experimental import pallas as pl
from jax.experimental.pallas import tpu as pltpu

f32 = jnp.float32
bf16 = jnp.bfloat16
i32 = jnp.int32

D_MODEL = 2048
EPS = 1e-6
ROPE_THETA = 500000.0
GLA_HEADS = 4
GLA_DK = 128
GLA_DV = 256
GLA_LOWRANK = 16
GLA_TAU = 16.0
GLA_CHUNK = 64
GLA_SUB = 8
NSA_HEADS = 16
NSA_GROUPS = 4
NSA_HPG = 4
NSA_DH = 128
ROPE_DIM = 32
CMP_BLOCK = 32
CMP_STRIDE = 16
SLC_BLOCK = 64
SLC_TOPN = 16
WINDOW = 512
PEER_HEADS = 8
PEER_NKEYS = 128
PEER_DKEY = 128
PEER_TOPK = 16

LANES = 128
OFF_NQ = 0
OFF_KCM = 2048
OFF_VCM = 2560
OFF_KSL = 3072
OFF_VSL = 3584
OFF_KWN = 4096
OFF_VWN = 4608
OFF_GQ = 5120
OFF_GK = 5632
OFF_GV = 6144
OFF_GR = 7168
OFF_GA = 8192
OFF_GB = 10240
OFF_NGATE = 12288
OFF_GLR = 12416
NCOLS = 12544

VMEM_LIMIT = 56 * 1024 * 1024
NEG = -1e30
HI = lax.Precision.HIGHEST


def _cp(sem, vmem=VMEM_LIMIT):
    return pltpu.CompilerParams(dimension_semantics=sem, vmem_limit_bytes=vmem)


def _nt(a, b, precision=None):
    return lax.dot_general(a, b, (((1,), (1,)), ((), ())), preferred_element_type=f32, precision=precision)


def _tn(a, b, precision=None):
    return lax.dot_general(a, b, (((0,), (0,)), ((), ())), preferred_element_type=f32, precision=precision)


def _dot(a, b, precision=None):
    return jnp.dot(a, b, preferred_element_type=f32, precision=precision)


def _proj_kernel(x_ref, g_ref, w_ref, o_ref, h_scr):
    @pl.when(pl.program_id(1) == 0)
    def _():
        x = x_ref[...]
        ms = jnp.mean(x * x, axis=-1, keepdims=True)
        h_scr[...] = (x * lax.rsqrt(ms + EPS) * g_ref[...]).astype(bf16)

    o_ref[...] = _dot(h_scr[...], w_ref[...])


def _proj(x2, g1, w_pack, tm, tn):
    T = x2.shape[0]
    return pl.pallas_call(
        _proj_kernel,
        out_shape=jax.ShapeDtypeStruct((T, NCOLS), f32),
        grid=(T // tm, NCOLS // tn),
        in_specs=[
            pl.BlockSpec((tm, D_MODEL), lambda i, j: (i, 0)),
            pl.BlockSpec((1, D_MODEL), lambda i, j: (0, 0)),
            pl.BlockSpec((D_MODEL, tn), lambda i, j: (0, j)),
        ],
        out_specs=pl.BlockSpec((tm, tn), lambda i, j: (i, j)),
        scratch_shapes=[pltpu.VMEM((tm, D_MODEL), bf16)],
        compiler_params=_cp(("parallel", "arbitrary")),
        name="proj",
    )(x2, g1, w_pack)


def _gla_kernel(gq_ref, gk_ref, gv_ref, gr_ref, glr_ref, wd_ref, bd_ref, ng_ref, tri_ref, o_ref, st_ref):
    C = GLA_CHUNK
    R = GLA_SUB

    @pl.when(pl.program_id(1) == 0)
    def _():
        st_ref[...] = jnp.zeros_like(st_ref)

    glr = glr_ref[...]
    tri = tri_ref[...]
    lane = lax.broadcasted_iota(i32, (R, C), 1)
    row = lax.broadcasted_iota(i32, (R, C), 0)
    for h in range(GLA_HEADS):
        ks = slice(h * GLA_DK, (h + 1) * GLA_DK)
        vs = slice(h * GLA_DV, (h + 1) * GLA_DV)
        z = _dot(glr, wd_ref[:, ks], HI) + bd_ref[:, ks]
        g = (jnp.minimum(z, 0.0) - jnp.log(1.0 + jnp.exp(-jnp.abs(z)))) / GLA_TAU
        b = _dot(tri, g, HI)
        q = gq_ref[:, ks] * (GLA_DK ** -0.5)
        k = gk_ref[:, ks]
        v = gv_ref[:, vs].astype(bf16)
        st = st_ref[h]
        o = _nt((q * jnp.exp(b)).astype(bf16), st.astype(bf16))
        blocks = []
        for i in range(C // R):
            r0 = i * R
            bi = b[r0:r0 + R]
            ref = b[r0:r0 + 1]
            qi = q[r0:r0 + R]
            qt = qi * jnp.exp(bi - ref)
            kt = k * jnp.exp(jnp.minimum(ref - b, 0.0))
            a = _nt(qt.astype(bf16), kt.astype(bf16))
            a = jnp.where(lane < r0, a, 0.0)
            for s in range(R):
                w = qi * k[r0 + s:r0 + s + 1] * jnp.exp(jnp.minimum(bi - b[r0 + s:r0 + s + 1], 0.0))
                col = jnp.sum(w, axis=-1, keepdims=True)
                a = jnp.where(lane == r0 + s, jnp.where(row >= s, col, 0.0), a)
            blocks.append(a)
        attn = jnp.concatenate(blocks, axis=0)
        o = o + _dot(attn.astype(bf16), v)
        bl = b[C - 1:C]
        kd = k * jnp.exp(bl - b)
        st_ref[h] = jnp.exp(bl) * st + _tn(v, kd.astype(bf16))
        ms = jnp.mean(o * o, axis=-1, keepdims=True)
        y = o * lax.rsqrt(ms + EPS) * ng_ref[...]
        r = gr_ref[:, vs]
        o_ref[:, vs] = y * (r * jax.nn.sigmoid(r))


def _gla(proj, wd_pad, bd, ng, B, S):
    C = GLA_CHUNK
    nc = S // C
    tri = jnp.asarray(np.tril(np.ones((C, C), np.float32)))
    qk_w = GLA_HEADS * GLA_DK
    v_w = GLA_HEADS * GLA_DV
    row = lambda b, c: b * nc + c
    return pl.pallas_call(
        _gla_kernel,
        out_shape=jax.ShapeDtypeStruct((B * S, v_w), f32),
        grid=(B, nc),
        in_specs=[
            pl.BlockSpec((C, qk_w), lambda b, c: (row(b, c), OFF_GQ // qk_w)),
            pl.BlockSpec((C, qk_w), lambda b, c: (row(b, c), OFF_GK // qk_w)),
            pl.BlockSpec((C, v_w), lambda b, c: (row(b, c), OFF_GV // v_w)),
            pl.BlockSpec((C, v_w), lambda b, c: (row(b, c), OFF_GR // v_w)),
            pl.BlockSpec((C, LANES), lambda b, c: (row(b, c), OFF_GLR // LANES)),
            pl.BlockSpec((LANES, qk_w), lambda b, c: (0, 0)),
            pl.BlockSpec((1, qk_w), lambda b, c: (0, 0)),
            pl.BlockSpec((1, GLA_DV), lambda b, c: (0, 0)),
            pl.BlockSpec((C, C), lambda b, c: (0, 0)),
        ],
        out_specs=pl.BlockSpec((C, v_w), lambda b, c: (row(b, c), 0)),
        scratch_shapes=[pltpu.VMEM((GLA_HEADS, GLA_DV, GLA_DK), f32)],
        compiler_params=_cp(("parallel", "arbitrary")),
        name="gla",
    )(proj, proj, proj, proj, proj, wd_pad, bd, ng, tri)


def _rope_kernel(q_ref, kc_ref, ks_ref, kw_ref, pos_ref, inv_ref, qo_ref, ko_ref):
    ang = pos_ref[...].astype(f32) * inv_ref[...]
    cos = jnp.cos(ang)
    sin = jnp.sin(ang)
    lane = lax.broadcasted_iota(i32, ang.shape, 1)
    half = ROPE_DIM // 2
    s_up = jnp.where((lane >= half) & (lane < ROPE_DIM), sin, 0.0)
    s_dn = jnp.where(lane < half, -sin, 0.0)

    def rot(x):
        return x * cos + pltpu.roll(x, half, 1) * s_up + pltpu.roll(x, LANES - half, 1) * s_dn

    for h in range(NSA_HEADS):
        sl = slice(h * NSA_DH, (h + 1) * NSA_DH)
        qo_ref[:, sl] = rot(q_ref[:, sl])
    kv_w = NSA_GROUPS * NSA_DH
    for n, ref in enumerate((kc_ref, ks_ref, kw_ref)):
        for g in range(NSA_GROUPS):
            sl = slice(g * NSA_DH, (g + 1) * NSA_DH)
            ko_ref[:, n * kv_w + g * NSA_DH:n * kv_w + (g + 1) * NSA_DH] = rot(ref[:, sl])


def _rope(proj, pos_col, inv_lane, tm):
    T = proj.shape[0]
    q_w = NSA_HEADS * NSA_DH
    kv_w = NSA_GROUPS * NSA_DH
    return pl.pallas_call(
        _rope_kernel,
        out_shape=(jax.ShapeDtypeStruct((T, q_w), f32), jax.ShapeDtypeStruct((T, 3 * kv_w), f32)),
        grid=(T // tm,),
        in_specs=[
            pl.BlockSpec((tm, q_w), lambda i: (i, OFF_NQ // q_w)),
            pl.BlockSpec((tm, kv_w), lambda i: (i, OFF_KCM // kv_w)),
            pl.BlockSpec((tm, kv_w), lambda i: (i, OFF_KSL // kv_w)),
            pl.BlockSpec((tm, kv_w), lambda i: (i, OFF_KWN // kv_w)),
            pl.BlockSpec((tm, 1), lambda i: (i, 0)),
            pl.BlockSpec((1, LANES), lambda i: (0, 0)),
        ],
        out_specs=(pl.BlockSpec((tm, q_w), lambda i: (i, 0)), pl.BlockSpec((tm, 3 * kv_w), lambda i: (i, 0))),
        compiler_params=_cp(("parallel",)),
        name="rope",
    )(proj, proj, proj, proj, pos_col, inv_lane)


def _compress_kernel(r_ref, pos_ref, w1_ref, w2_ref, o_ref):
    half = (CMP_BLOCK // 2) * NSA_DH
    r = r_ref[0]
    n_rows = r.shape[0]
    ra = (r + pos_ref[0:1, :]).astype(bf16)
    rb = (pltpu.roll(r, n_rows - 1, 0) + pos_ref[1:2, :]).astype(bf16)
    hid = _dot(ra, w1_ref[0:half, :]) + _dot(rb, w1_ref[half:2 * half, :])
    o_ref[0] = _dot(jax.nn.gelu(hid).astype(bf16), w2_ref[...])


def _compress(r, pos2, w1, w2):
    BG, n_rows, width = r.shape
    return pl.pallas_call(
        _compress_kernel,
        out_shape=jax.ShapeDtypeStruct((BG, n_rows, NSA_DH), f32),
        grid=(BG,),
        in_specs=[
            pl.BlockSpec((1, n_rows, width), lambda i: (i, 0, 0)),
            pl.BlockSpec((2, width), lambda i: (0, 0)),
            pl.BlockSpec((2 * width, NSA_DH), lambda i: (0, 0)),
            pl.BlockSpec((NSA_DH, NSA_DH), lambda i: (0, 0)),
        ],
        out_specs=pl.BlockSpec((1, n_rows, NSA_DH), lambda i: (i, 0, 0)),
        compiler_params=_cp(("parallel",)),
        name="compress",
    )(r, pos2, w1, w2)


def _cmp_sel_kernel(q_ref, kc_ref, vc_ref, at_ref, o_ref, sel_ref, *, tq, n_slc):
    i = pl.program_id(2)
    ncp = kc_ref.shape[1]
    scale = NSA_DH ** -0.5
    t = i * tq + lax.broadcasted_iota(i32, (tq, ncp), 0)
    n_end = lax.broadcasted_iota(i32, (tq, ncp), 1) * CMP_STRIDE + (CMP_BLOCK - 1)
    cmask = n_end <= t
    kc = kc_ref[0].astype(bf16)
    vc = vc_ref[0].astype(bf16)
    pg = jnp.zeros((tq, ncp), f32)
    for hp in range(NSA_HPG):
        sl = slice(hp * NSA_DH, (hp + 1) * NSA_DH)
        s = _nt(q_ref[:, sl].astype(bf16), kc) * scale
        s = jnp.where(cmask, s, NEG)
        m = jnp.max(s, axis=-1, keepdims=True)
        p = jnp.where(cmask, jnp.exp(s - m), 0.0)
        p = p / jnp.maximum(jnp.sum(p, axis=-1, keepdims=True), 1e-30)
        o_ref[:, sl] = _dot(p.astype(bf16), vc)
        pg = pg + p
    imp = _nt(at_ref[...], pg, HI)
    j = lax.broadcasted_iota(i32, (n_slc, tq), 0)
    blk = (i * tq + lax.broadcasted_iota(i32, (n_slc, tq), 1)) // SLC_BLOCK
    forced = (j == 0) | (j == blk) | (j == blk - 1)
    imp = jnp.where(forced, 1e9, jnp.where(j > blk, -1e9, imp))
    sel = jnp.zeros((n_slc, tq), f32)
    cur = imp
    for _ in range(min(SLC_TOPN, n_slc)):
        m = jnp.max(cur, axis=0, keepdims=True)
        first = jnp.min(jnp.where(cur == m, j, n_slc), axis=0, keepdims=True)
        pick = j == first
        sel = jnp.where(pick, 1.0, sel)
        cur = jnp.where(pick, -jnp.inf, cur)
    sel_ref[0] = sel


def _cmp_sel(q_rot, kc, vc, at, B, S, tq):
    nq = S // tq
    G = NSA_GROUPS
    ncp = kc.shape[1]
    n_slc = S // SLC_BLOCK
    gw = NSA_HPG * NSA_DH
    return pl.pallas_call(
        functools.partial(_cmp_sel_kernel, tq=tq, n_slc=n_slc),
        out_shape=(jax.ShapeDtypeStruct((B * S, NSA_HEADS * NSA_DH), f32),
                   jax.ShapeDtypeStruct((B * G, n_slc, S), f32)),
        grid=(B, G, nq),
        in_specs=[
            pl.BlockSpec((tq, gw), lambda b, g, i: (b * nq + i, g)),
            pl.BlockSpec((1, ncp, NSA_DH), lambda b, g, i: (b * G + g, 0, 0)),
            pl.BlockSpec((1, ncp, NSA_DH), lambda b, g, i: (b * G + g, 0, 0)),
            pl.BlockSpec((n_slc, ncp), lambda b, g, i: (0, 0)),
        ],
        out_specs=(pl.BlockSpec((tq, gw), lambda b, g, i: (b * nq + i, g)),
                   pl.BlockSpec((1, n_slc, tq), lambda b, g, i: (b * G + g, 0, i))),
        compiler_params=_cp(("parallel", "parallel", "parallel")),
        name="cmp_sel",
    )(q_rot, kc, vc, at)


def _slc_kernel(qi_ref, kj_ref, q_ref, k_ref, v_ref, sel_ref, o_ref, m_scr, l_scr, acc_scr, *, tq, tk, n_slc):
    p_id = pl.program_id(2)
    i = qi_ref[p_id]
    j = kj_ref[p_id]
    scale = NSA_DH ** -0.5

    @pl.when(j == 0)
    def _():
        m_scr[...] = jnp.full_like(m_scr, NEG)
        l_scr[...] = jnp.zeros_like(l_scr)
        acc_scr[...] = jnp.zeros_like(acc_scr)

    kblk = (j * tk + lax.broadcasted_iota(i32, (n_slc, tk), 1)) // SLC_BLOCK
    expand = jnp.where(kblk == lax.broadcasted_iota(i32, (n_slc, tk), 0), 1.0, 0.0).astype(bf16)
    picked = _dot(sel_ref[0].astype(bf16), expand)
    t = i * tq + lax.broadcasted_iota(i32, (tq, tk), 0)
    kpos = j * tk + lax.broadcasted_iota(i32, (tq, tk), 1)
    mask = jnp.where(kpos <= t, picked, 0.0) > 0.5
    k = k_ref[...].astype(bf16)
    v = v_ref[...].astype(bf16)
    for hp in range(NSA_HPG):
        sl = slice(hp * NSA_DH, (hp + 1) * NSA_DH)
        s = _nt(q_ref[:, sl].astype(bf16), k) * scale
        s = jnp.where(mask, s, NEG)
        m_old = m_scr[hp]
        m_new = jnp.maximum(m_old, jnp.max(s, axis=-1, keepdims=True))
        alpha = jnp.exp(m_old - m_new)
        p = jnp.exp(s - m_new)
        l_scr[hp] = alpha * l_scr[hp] + jnp.sum(p, axis=-1, keepdims=True)
        acc_scr[hp] = alpha * acc_scr[hp] + _dot(p.astype(bf16), v)
        m_scr[hp] = m_new

    @pl.when(j == ((i + 1) * tq - 1) // tk)
    def _():
        for hp in range(NSA_HPG):
            sl = slice(hp * NSA_DH, (hp + 1) * NSA_DH)
            o_ref[:, sl] = acc_scr[hp] / jnp.maximum(l_scr[hp], 1e-30)


def _slc(q_rot, k_rot, proj, sel_tm, B, S, tq, tk):
    nq, nk = S // tq, S // tk
    G = NSA_GROUPS
    n_slc = S // SLC_BLOCK
    gw = NSA_HPG * NSA_DH
    kv_w = NSA_GROUPS * NSA_DH
    pairs = [(i, j) for i in range(nq) for j in range(((i + 1) * tq - 1) // tk + 1)]
    qi = jnp.asarray([p[0] for p in pairs], i32)
    kj = jnp.asarray([p[1] for p in pairs], i32)
    grid_spec = pltpu.PrefetchScalarGridSpec(
        num_scalar_prefetch=2,
        grid=(B, G, len(pairs)),
        in_specs=[
            pl.BlockSpec((tq, gw), lambda b, g, p, qi, kj: (b * nq + qi[p], g)),
            pl.BlockSpec((tk, NSA_DH), lambda b, g, p, qi, kj: (b * nk + kj[p], kv_w // NSA_DH + g)),
            pl.BlockSpec((tk, NSA_DH), lambda b, g, p, qi, kj: (b * nk + kj[p], OFF_VSL // NSA_DH + g)),
            pl.BlockSpec((1, tq, n_slc), lambda b, g, p, qi, kj: (b * G + g, qi[p], 0)),
        ],
        out_specs=pl.BlockSpec((tq, gw), lambda b, g, p, qi, kj: (b * nq + qi[p], g)),
        scratch_shapes=[pltpu.VMEM((NSA_HPG, tq, 1), f32), pltpu.VMEM((NSA_HPG, tq, 1), f32),
                        pltpu.VMEM((NSA_HPG, tq, NSA_DH), f32)],
    )
    return pl.pallas_call(
        functools.partial(_slc_kernel, tq=tq, tk=tk, n_slc=n_slc),
        out_shape=jax.ShapeDtypeStruct((B * S, NSA_HEADS * NSA_DH), f32),
        grid_spec=grid_spec,
        compiler_params=_cp(("parallel", "parallel", "arbitrary")),
        name="slc",
    )(qi, kj, q_rot, k_rot, proj, sel_tm)


def _win_kernel(q_ref, k0_ref, k1_ref, k2_ref, v0_ref, v1_ref, v2_ref, o_ref, *, tq):
    i = pl.program_id(2)
    scale = NSA_DH ** -0.5
    nb = WINDOW // tq + 1
    k = jnp.concatenate([k0_ref[...], k1_ref[...], k2_ref[...]], axis=0).astype(bf16)
    v = jnp.concatenate([v0_ref[...], v1_ref[...], v2_ref[...]], axis=0).astype(bf16)
    t = i * tq + lax.broadcasted_iota(i32, (tq, nb * tq), 0)
    kpos = (i - (nb - 1)) * tq + lax.broadcasted_iota(i32, (tq, nb * tq), 1)
    mask = jnp.where(kpos >= 0, jnp.where(kpos <= t, jnp.where(kpos > t - WINDOW, 1.0, 0.0), 0.0), 0.0) > 0.5
    for hp in range(NSA_HPG):
        sl = slice(hp * NSA_DH, (hp + 1) * NSA_DH)
        s = _nt(q_ref[:, sl].astype(bf16), k) * scale
        s = jnp.where(mask, s, NEG)
        m = jnp.max(s, axis=-1, keepdims=True)
        p = jnp.where(mask, jnp.exp(s - m), 0.0)
        p = p / jnp.maximum(jnp.sum(p, axis=-1, keepdims=True), 1e-30)
        o_ref[:, sl] = _dot(p.astype(bf16), v)


def _win(q_rot, k_rot, proj, B, S, tq):
    assert WINDOW % tq == 0 and WINDOW // tq == 2
    nq = S // tq
    gw = NSA_HPG * NSA_DH
    kv_w = NSA_GROUPS * NSA_DH

    def kspec(d, col0):
        return pl.BlockSpec((tq, NSA_DH), lambda b, g, i: (b * nq + jnp.maximum(i - 2 + d, 0), col0 + g))

    return pl.pallas_call(
        functools.partial(_win_kernel, tq=tq),
        out_shape=jax.ShapeDtypeStruct((B * S, NSA_HEADS * NSA_DH), f32),
        grid=(B, NSA_GROUPS, nq),
        in_specs=[pl.BlockSpec((tq, gw), lambda b, g, i: (b * nq + i, g))]
        + [kspec(d, 2 * kv_w // NSA_DH) for d in range(3)]
        + [kspec(d, OFF_VWN // NSA_DH) for d in range(3)],
        out_specs=pl.BlockSpec((tq, gw), lambda b, g, i: (b * nq + i, g)),
        compiler_params=_cp(("parallel", "parallel", "parallel")),
        name="win",
    )(q_rot, k_rot, k_rot, k_rot, proj, proj, proj)


def _merge_kernel(yg_ref, oc_ref, os_ref, ow_ref, ng_ref, ga_ref, gb_ref, wg_ref, wn_ref, o_ref, yn_scr):
    sg = jax.nn.sigmoid(ng_ref[...])
    for h in range(NSA_HEADS):
        sl = slice(h * NSA_DH, (h + 1) * NSA_DH)
        y = (sg[:, 3 * h:3 * h + 1] * oc_ref[:, sl] + sg[:, 3 * h + 1:3 * h + 2] * os_ref[:, sl]
             + sg[:, 3 * h + 2:3 * h + 3] * ow_ref[:, sl])
        yn_scr[:, sl] = y.astype(bf16)
    up_g = _dot(yg_ref[...].astype(bf16), wg_ref[...])
    up_n = _dot(yn_scr[...], wn_ref[...])
    o_ref[...] = (jax.nn.sigmoid(ga_ref[...]) * up_g + jax.nn.sigmoid(gb_ref[...]) * up_n).astype(o_ref.dtype)


def _merge(y_gla, o_cmp, o_slc, o_win, proj, wg, wn, tm):
    T = y_gla.shape[0]
    qw = NSA_HEADS * NSA_DH
    gvw = GLA_HEADS * GLA_DV
    const = dict(pipeline_mode=pl.Buffered(1))
    return pl.pallas_call(
        _merge_kernel,
        out_shape=jax.ShapeDtypeStruct((T, D_MODEL), bf16),
        grid=(T // tm,),
        in_specs=[
            pl.BlockSpec((tm, gvw), lambda i: (i, 0)),
            pl.BlockSpec((tm, qw), lambda i: (i, 0)),
            pl.BlockSpec((tm, qw), lambda i: (i, 0)),
            pl.BlockSpec((tm, qw), lambda i: (i, 0)),
            pl.BlockSpec((tm, LANES), lambda i: (i, OFF_NGATE // LANES)),
            pl.BlockSpec((tm, D_MODEL), lambda i: (i, OFF_GA // D_MODEL)),
            pl.BlockSpec((tm, D_MODEL), lambda i: (i, OFF_GB // D_MODEL)),
            pl.BlockSpec((gvw, D_MODEL), lambda i: (0, 0), **const),
            pl.BlockSpec((qw, D_MODEL), lambda i: (0, 0), **const),
        ],
        out_specs=pl.BlockSpec((tm, D_MODEL), lambda i: (i, 0)),
        scratch_shapes=[pltpu.VMEM((tm, qw), bf16)],
        compiler_params=_cp(("parallel",)),
        name="merge",
    )(y_gla, o_cmp, o_slc, o_win, proj, proj, proj, wg, wn)


def _outp_kernel(m_ref, x_ref, wo_ref, g2_ref, wq_ref, x1_ref, h2_ref, qp_ref):
    x1 = x_ref[...] + _dot(m_ref[...], wo_ref[...])
    x1_ref[...] = x1
    ms = jnp.mean(x1 * x1, axis=-1, keepdims=True)
    h2 = x1 * lax.rsqrt(ms + EPS) * g2_ref[...]
    h2_ref[...] = h2
    qp_ref[...] = _dot(h2.astype(bf16), wq_ref[...])


def _outp(merged, x2, wo, g2, wq, tm):
    T = x2.shape[0]
    pw = PEER_HEADS * PEER_DKEY
    const = dict(pipeline_mode=pl.Buffered(1))
    return pl.pallas_call(
        _outp_kernel,
        out_shape=(jax.ShapeDtypeStruct((T, D_MODEL), f32), jax.ShapeDtypeStruct((T, D_MODEL), f32),
                   jax.ShapeDtypeStruct((T, pw), f32)),
        grid=(T // tm,),
        in_specs=[
            pl.BlockSpec((tm, D_MODEL), lambda i: (i, 0)),
            pl.BlockSpec((tm, D_MODEL), lambda i: (i, 0)),
            pl.BlockSpec((D_MODEL, D_MODEL), lambda i: (0, 0), **const),
            pl.BlockSpec((1, D_MODEL), lambda i: (0, 0)),
            pl.BlockSpec((D_MODEL, pw), lambda i: (0, 0), **const),
        ],
        out_specs=(pl.BlockSpec((tm, D_MODEL), lambda i: (i, 0)), pl.BlockSpec((tm, D_MODEL), lambda i: (i, 0)),
                   pl.BlockSpec((tm, pw), lambda i: (i, 0))),
        compiler_params=_cp(("parallel",)),
        name="outp",
    )(merged, x2, wo, g2, wq)


def _topk_rows(cur, iota0, n, k, payload=None):
    vals, outs = [], []
    for _ in range(k):
        m = jnp.max(cur, axis=0, keepdims=True)
        first = jnp.min(jnp.where(cur == m, iota0, n), axis=0, keepdims=True)
        pick = iota0 == first
        vals.append(m)
        outs.append(first if payload is None else jnp.sum(jnp.where(pick, payload, 0), axis=0, keepdims=True))
        cur = jnp.where(pick, -jnp.inf, cur)
    return jnp.concatenate(vals, axis=0), jnp.concatenate(outs, axis=0)


_PEER_CELLS = [(a, b) for a in range(PEER_TOPK) for b in range(PEER_TOPK // (a + 1))]
_PEER_NCELL = -(-len(_PEER_CELLS) // 8) * 8


def _ptopk_kernel(qp_ref, k1_ref, k2_ref, e_ref, g_ref, *, tt):
    K = PEER_TOPK
    half = PEER_DKEY // 2
    io_keys = lax.broadcasted_iota(i32, (PEER_NKEYS, tt), 0)
    io_cand = lax.broadcasted_iota(i32, (_PEER_NCELL, tt), 0)
    starts = [r for r, (a, b) in enumerate(_PEER_CELLS) if b == 0]
    a_row = jnp.zeros((_PEER_NCELL, tt), i32)
    start_row = jnp.zeros((_PEER_NCELL, tt), i32)
    for st in starts[1:]:
        a_row = a_row + jnp.where(io_cand >= st, 1, 0)
        start_row = jnp.where(io_cand >= st, st, start_row)
    b_row = io_cand - start_row
    real = io_cand < len(_PEER_CELLS)
    for h in range(PEER_HEADS):
        q1 = qp_ref[:, h * PEER_DKEY:h * PEER_DKEY + half]
        q2 = qp_ref[:, h * PEER_DKEY + half:(h + 1) * PEER_DKEY]
        s1 = _nt(k1_ref[h], q1, HI)
        s2 = _nt(k2_ref[h], q2, HI)
        v1, i1 = _topk_rows(s1, io_keys, PEER_NKEYS, K)
        v2, i2 = _topk_rows(s2, io_keys, PEER_NKEYS, K)
        va = jnp.zeros((_PEER_NCELL, tt), f32)
        vb = jnp.zeros((_PEER_NCELL, tt), f32)
        ea = jnp.zeros((_PEER_NCELL, tt), i32)
        eb = jnp.zeros((_PEER_NCELL, tt), i32)
        for r in range(K):
            va = jnp.where(a_row == r, v1[r:r + 1], va)
            ea = jnp.where(a_row == r, i1[r:r + 1], ea)
            vb = jnp.where(b_row == r, v2[r:r + 1], vb)
            eb = jnp.where(b_row == r, i2[r:r + 1], eb)
        cand = jnp.where(real, va + vb, -jnp.inf)
        ecand = ea * PEER_NKEYS + eb
        vals, experts = _topk_rows(cand, io_cand, _PEER_NCELL, K, payload=ecand)
        p = jnp.exp(vals - vals[0:1])
        g_ref[h] = p / jnp.sum(p, axis=0, keepdims=True)
        e_ref[h] = experts


def _ptopk(qp, k1, k2, tt):
    T = qp.shape[0]
    K = PEER_TOPK
    pw = PEER_HEADS * PEER_DKEY
    return pl.pallas_call(
        functools.partial(_ptopk_kernel, tt=tt),
        out_shape=(jax.ShapeDtypeStruct((PEER_HEADS, K, T), i32), jax.ShapeDtypeStruct((PEER_HEADS, K, T), f32)),
        grid=(T // tt,),
        in_specs=[
            pl.BlockSpec((tt, pw), lambda i: (i, 0)),
            pl.BlockSpec((PEER_HEADS, PEER_NKEYS, PEER_DKEY // 2), lambda i: (0, 0, 0)),
            pl.BlockSpec((PEER_HEADS, PEER_NKEYS, PEER_DKEY // 2), lambda i: (0, 0, 0)),
        ],
        out_specs=(pl.BlockSpec((PEER_HEADS, K, tt), lambda i: (0, 0, i)),
                   pl.BlockSpec((PEER_HEADS, K, tt), lambda i: (0, 0, i))),
        compiler_params=_cp(("parallel",)),
        name="ptopk",
    )(qp, k1, k2)


PEER_ROWS = D_MODEL // LANES
PEER_SEL = PEER_HEADS * PEER_TOPK


def _pack_expert_table(peer_u, peer_v):
    ub = lax.bitcast_convert_type(peer_u.astype(bf16), jnp.uint16).astype(jnp.uint32)
    vb = lax.bitcast_convert_type(peer_v.astype(bf16), jnp.uint16).astype(jnp.uint32)
    return (ub | (vb << 16)).reshape(-1, PEER_ROWS, LANES)


def _pexp_kernel(idx_ref, idx_next_ref, g_ref, x_ref, x1_ref, gf_ref, tab_ref, o_ref, a0, a1, b0, b1, sem, *, tt):
    i = pl.program_id(0)
    n_steps = pl.num_programs(0)
    th = tt // 2

    def start(src_idx_ref, t, k, halves, sems):
        e = src_idx_ref[0, 0, t * PEER_SEL + k]
        h, tl = divmod(t, th)
        pltpu.make_async_copy(tab_ref.at[e], halves[h].at[:, tl * PEER_SEL + k, :], sems[h]).start(priority=k % 2)

    def wait_all(dst, dst_sem):
        pltpu.make_async_copy(dst, dst, dst_sem).wait()

    def compute(src, tl, t):
        rows = slice(tl * PEER_SEL, (tl + 1) * PEER_SEL)
        acc = jnp.zeros((PEER_SEL, LANES), f32)
        for s in range(PEER_ROWS):
            u = pltpu.bitcast(src[s, rows, :] << 16, f32)
            acc = acc + u * x_ref[0, t:t + 1, s * LANES:(s + 1) * LANES]
        hid = jnp.sum(acc.T, axis=0, keepdims=True)
        w = (g_ref[0, t:t + 1, :] * jax.nn.gelu(hid)).astype(bf16)
        for s in range(PEER_ROWS):
            v = pltpu.bitcast(src[s, rows, :] & jnp.uint32(0xFFFF0000), f32).astype(bf16)
            o_ref[0, t:t + 1, s * LANES:(s + 1) * LANES] = _dot(w, v)

    def step(cur, cur_sems, nxt, nxt_sems):
        for h in range(2):
            wait_all(cur[h], cur_sems[h])
            for tl in range(th):
                t = h * th + tl
                for k in range(PEER_SEL):
                    start(idx_next_ref, t, k, nxt, nxt_sems)
                compute(cur[h], tl, t)
        y = x1_ref[0] + o_ref[0]
        ms = jnp.mean(y * y, axis=-1, keepdims=True)
        o_ref[0] = y * lax.rsqrt(ms + EPS) * gf_ref[...]

    bufs_a, sems_a = (a0, a1), (sem.at[0], sem.at[1])
    bufs_b, sems_b = (b0, b1), (sem.at[2], sem.at[3])

    @pl.when(i == 0)
    def _():
        for h in range(2):
            def body(n, carry, h=h):
                e = idx_ref[0, 0, h * th * PEER_SEL + n]
                pltpu.make_async_copy(tab_ref.at[e], bufs_a[h].at[:, n, :], sems_a[h]).start()
                return carry
            lax.fori_loop(0, th * PEER_SEL, body, 0, unroll=8)

    @pl.when(lax.rem(i, 2) == 0)
    def _():
        step(bufs_a, sems_a, bufs_b, sems_b)

    @pl.when(lax.rem(i, 2) == 1)
    def _():
        step(bufs_b, sems_b, bufs_a, sems_a)

    @pl.when((i == n_steps - 1) & (lax.rem(i, 2) == 0))
    def _():
        wait_all(b0, sems_b[0])
        wait_all(b1, sems_b[1])

    @pl.when((i == n_steps - 1) & (lax.rem(i, 2) == 1))
    def _():
        wait_all(a0, sems_a[0])
        wait_all(a1, sems_a[1])


def _pexp(idx3, gates, h2_3d, x1_3d, gf, table, tt):
    n_steps = gates.shape[0]
    rows = pl.BlockSpec((1, tt, D_MODEL), lambda i: (i, 0, 0))
    return pl.pallas_call(
        functools.partial(_pexp_kernel, tt=tt),
        out_shape=jax.ShapeDtypeStruct((n_steps, tt, D_MODEL), f32),
        grid=(n_steps,),
        in_specs=[
            pl.BlockSpec((1, 1, tt * PEER_SEL), lambda i: (i, 0, 0), memory_space=pltpu.SMEM),
            pl.BlockSpec((1, 1, tt * PEER_SEL), lambda i: (jnp.minimum(i + 1, n_steps - 1), 0, 0),
                         memory_space=pltpu.SMEM),
            pl.BlockSpec((1, tt, PEER_SEL), lambda i: (i, 0, 0)),
            rows,
            rows,
            pl.BlockSpec((1, D_MODEL), lambda i: (0, 0)),
            pl.BlockSpec(memory_space=pl.ANY),
        ],
        out_specs=rows,
        scratch_shapes=[pltpu.VMEM((PEER_ROWS, (tt // 2) * PEER_SEL, LANES), jnp.uint32) for _ in range(4)]
        + [pltpu.SemaphoreType.DMA((4,))],
        compiler_params=_cp(("arbitrary",)),
        name="pexp",
    )(idx3, idx3, gates, h2_3d, x1_3d, gf, table)


def _pack_w_in(w):
    zpad = lambda a, n: jnp.pad(a, ((0, 0), (0, n - a.shape[1])))
    return jnp.concatenate(
        [w[:, 0:5120], w[:, 5168:7216], w[:, 7232:12352], zpad(w[:, 5120:5168], LANES), zpad(w[:, 7216:7232], LANES)],
        axis=1).astype(bf16)


def _importance_map(n_cmp_pad, n_slc):
    ratio = SLC_BLOCK // CMP_STRIDE
    nrep = CMP_BLOCK // CMP_STRIDE
    at = np.zeros((n_slc, n_cmp_pad), np.float32)
    for j in range(n_slc):
        for m in range(ratio):
            for n in range(nrep):
                c = ratio * j + m - n
                if 0 <= c < n_cmp_pad:
                    at[j, c] += 1.0
    return jnp.asarray(at)


def _mixers(x2, positions, norm1_g, w_in, gla_w_decay, gla_b_decay, gla_norm_g, pos_k, pos_v, k_w1, k_w2, v_w1, v_w2,
            B, S):
    T = B * S
    G = NSA_GROUPS
    proj = _proj(x2, norm1_g.reshape(1, D_MODEL), _pack_w_in(w_in), tm=min(1024, T), tn=896)

    wd_pad = jnp.pad(gla_w_decay, ((0, LANES - GLA_LOWRANK), (0, 0)))
    y_gla = _gla(proj, wd_pad, gla_b_decay.reshape(1, -1), gla_norm_g.reshape(1, GLA_DV), B, S)

    inv = jnp.power(ROPE_THETA, -(jnp.arange(0, ROPE_DIM, 2, dtype=f32) / ROPE_DIM))
    inv_lane = jnp.concatenate([inv, inv, jnp.zeros((LANES - ROPE_DIM,), f32)]).reshape(1, LANES)
    q_rot, k_rot = _rope(proj, positions.reshape(T, 1), inv_lane, tm=min(512, T))

    n_rows = S // CMP_STRIDE
    kv_w = G * NSA_DH

    def stride_rows(a):
        return a.reshape(B, S, G, NSA_DH).transpose(0, 2, 1, 3).reshape(B * G, n_rows, CMP_STRIDE * NSA_DH)

    r_k = stride_rows(k_rot[:, 0:kv_w])
    r_v = stride_rows(proj[:, OFF_VCM:OFF_VCM + kv_w])
    pos2 = lambda p: p.reshape(2, (CMP_BLOCK // 2) * NSA_DH)
    kc = _compress(r_k, pos2(pos_k), k_w1.astype(bf16), k_w2.astype(bf16))
    vc = _compress(r_v, pos2(pos_v), v_w1.astype(bf16), v_w2.astype(bf16))

    n_slc = S // SLC_BLOCK
    o_cmp, sel_t = _cmp_sel(q_rot, kc, vc, _importance_map(n_rows, n_slc), B, S, tq=min(256, S))
    sel_tm = sel_t.transpose(0, 2, 1)
    o_slc = _slc(q_rot, k_rot, proj, sel_tm, B, S, tq=256, tk=min(512, S))
    o_win = _win(q_rot, k_rot, proj, B, S, tq=256)
    return proj, y_gla, o_cmp, o_slc, o_win


def _peer_stage(x1, h2, qp, sub_k1, sub_k2, peer_u, peer_v, final_g):
    T = x1.shape[0]
    e_t, g_t = _ptopk(qp, sub_k1, sub_k2, tt=256)
    tt = 16
    idx3 = e_t.transpose(2, 0, 1).reshape(T // tt, 1, tt * PEER_SEL)
    gates = g_t.transpose(2, 0, 1).reshape(T // tt, tt, PEER_SEL)
    table = _pack_expert_table(peer_u, peer_v)
    blocks = lambda a: a.reshape(T // tt, tt, D_MODEL)
    out = _pexp(idx3, gates, blocks(h2), blocks(x1), final_g.reshape(1, D_MODEL), table, tt)
    return out.reshape(T, D_MODEL)


def kernel(x, positions, norm1_g, w_in, gla_w_decay, gla_b_decay, gla_norm_g, nsa_cmp_pos_k, nsa_cmp_pos_v,
           nsa_cmp_k_w1, nsa_cmp_k_w2, nsa_cmp_v_w1, nsa_cmp_v_w2, w_up_gla, w_up_nsa, w_out, norm2_g,
           peer_w_q, peer_sub_k1, peer_sub_k2, peer_u, peer_v, final_g):
    B, S, D = x.shape
    T = B * S
    assert norm1_g.shape[0] == 1 and D == D_MODEL
    x2 = x.reshape(T, D)
    proj, y_gla, o_cmp, o_slc, o_win = _mixers(
        x2, positions, norm1_g[0], w_in[0], gla_w_decay[0], gla_b_decay[0], gla_norm_g[0], nsa_cmp_pos_k[0],
        nsa_cmp_pos_v[0], nsa_cmp_k_w1[0], nsa_cmp_k_w2[0], nsa_cmp_v_w1[0], nsa_cmp_v_w2[0], B, S)
    merged = _merge(y_gla, o_cmp, o_slc, o_win, proj, w_up_gla[0].astype(bf16), w_up_nsa[0].astype(bf16), tm=256)
    x1, h2, qp = _outp(merged, x2, w_out[0].astype(bf16), norm2_g[0].reshape(1, D), peer_w_q[0].astype(bf16), tm=256)
    out = _peer_stage(x1, h2, qp, peer_sub_k1[0], peer_sub_k2[0], peer_u[0], peer_v[0], final_g)
    return out.reshape(B, S, D)
```

```python
import functools

import jax
import jax.numpy as jnp
import numpy as np
from jax import lax
from jax.experimental import pallas as pl
from jax.experimental.pallas import tpu as pltpu

f32 = jnp.float32
bf16 = jnp.bfloat16
i32 = jnp.int32

D_MODEL = 2048
EPS = 1e-6
ROPE_THETA = 500000.0
GLA_HEADS = 4
GLA_DK = 128
GLA_DV = 256
GLA_LOWRANK = 16
GLA_TAU = 16.0
GLA_CHUNK = 64
GLA_SUB = 8
NSA_HEADS = 16
NSA_GROUPS = 4
NSA_HPG = 4
NSA_DH = 128
ROPE_DIM = 32
CMP_BLOCK = 32
CMP_STRIDE = 16
SLC_BLOCK = 64
SLC_TOPN = 16
WINDOW = 512
PEER_HEADS = 8
PEER_NKEYS = 128
PEER_DKEY = 128
PEER_TOPK = 16

LANES = 128
OFF_NQ = 0
OFF_KCM = 2048
OFF_VCM = 2560
OFF_KSL = 3072
OFF_VSL = 3584
OFF_KWN = 4096
OFF_VWN = 4608
OFF_GQ = 5120
OFF_GK = 5632
OFF_GV = 6144
OFF_GR = 7168
OFF_GA = 8192
OFF_GB = 10240
OFF_NGATE = 12288
OFF_GLR = 12416
NCOLS = 12544

VMEM_LIMIT = 56 * 1024 * 1024
NEG = -1e30
HI = lax.Precision.HIGHEST


def _cp(sem, vmem=VMEM_LIMIT):
    return pltpu.CompilerParams(dimension_semantics=sem, vmem_limit_bytes=vmem)


def _nt(a, b, precision=None):
    return lax.dot_general(a, b, (((1,), (1,)), ((), ())), preferred_element_type=f32, precision=precision)


def _tn(a, b, precision=None):
    return lax.dot_general(a, b, (((0,), (0,)), ((), ())), preferred_element_type=f32, precision=precision)


def _dot(a, b, precision=None):
    return jnp.dot(a, b, preferred_element_type=f32, precision=precision)


def _proj_kernel(x_ref, g_ref, w_ref, o_ref, h_scr):
    @pl.when(pl.program_id(1) == 0)
    def _():
        x = x_ref[...]
        ms = jnp.mean(x * x, axis=-1, keepdims=True)
        h_scr[...] = (x * lax.rsqrt(ms + EPS) * g_ref[...]).astype(bf16)

    o_ref[...] = _dot(h_scr[...], w_ref[...])


def _proj(x2, g1, w_pack, tm, tn):
    T = x2.shape[0]
    return pl.pallas_call(
        _proj_kernel,
        out_shape=jax.ShapeDtypeStruct((T, NCOLS), f32),
        grid=(T // tm, NCOLS // tn),
        in_specs=[
            pl.BlockSpec((tm, D_MODEL), lambda i, j: (i, 0)),
            pl.BlockSpec((1, D_MODEL), lambda i, j: (0, 0)),
            pl.BlockSpec((D_MODEL, tn), lambda i, j: (0, j)),
        ],
        out_specs=pl.BlockSpec((tm, tn), lambda i, j: (i, j)),
        scratch_shapes=[pltpu.VMEM((tm, D_MODEL), bf16)],
        compiler_params=_cp(("parallel", "arbitrary")),
        name="proj",
    )(x2, g1, w_pack)


def _gla_kernel(gq_ref, gk_ref, gv_ref, gr_ref, glr_ref, wd_ref, bd_ref, ng_ref, tri_ref, o_ref, st_ref):
    C = GLA_CHUNK
    R = GLA_SUB

    @pl.when(pl.program_id(1) == 0)
    def _():
        st_ref[...] = jnp.zeros_like(st_ref)

    glr = glr_ref[...]
    tri = tri_ref[...]
    lane = lax.broadcasted_iota(i32, (R, C), 1)
    row = lax.broadcasted_iota(i32, (R, C), 0)
    for h in range(GLA_HEADS):
        ks = slice(h * GLA_DK, (h + 1) * GLA_DK)
        vs = slice(h * GLA_DV, (h + 1) * GLA_DV)
        z = _dot(glr, wd_ref[:, ks], HI) + bd_ref[:, ks]
        g = (jnp.minimum(z, 0.0) - jnp.log(1.0 + jnp.exp(-jnp.abs(z)))) / GLA_TAU
        b = _dot(tri, g, HI)
        q = gq_ref[:, ks] * (GLA_DK ** -0.5)
        k = gk_ref[:, ks]
        v = gv_ref[:, vs].astype(bf16)
        st = st_ref[h]
        o = _nt((q * jnp.exp(b)).astype(bf16), st.astype(bf16))
        blocks = []
        for i in range(C // R):
            r0 = i * R
            bi = b[r0:r0 + R]
            ref = b[r0:r0 + 1]
            qi = q[r0:r0 + R]
            qt = qi * jnp.exp(bi - ref)
            kt = k * jnp.exp(jnp.minimum(ref - b, 0.0))
            a = _nt(qt.astype(bf16), kt.astype(bf16))
            a = jnp.where(lane < r0, a, 0.0)
            for s in range(R):
                w = qi * k[r0 + s:r0 + s + 1] * jnp.exp(jnp.minimum(bi - b[r0 + s:r0 + s + 1], 0.0))
                col = jnp.sum(w, axis=-1, keepdims=True)
                a = jnp.where(lane == r0 + s, jnp.where(row >= s, col, 0.0), a)
            blocks.append(a)
        attn = jnp.concatenate(blocks, axis=0)
        o = o + _dot(attn.astype(bf16), v)
        bl = b[C - 1:C]
        kd = k * jnp.exp(bl - b)
        st_ref[h] = jnp.exp(bl) * st + _tn(v, kd.astype(bf16))
        ms = jnp.mean(o * o, axis=-1, keepdims=True)
        y = o * lax.rsqrt(ms + EPS) * ng_ref[...]
        r = gr_ref[:, vs]
        o_ref[:, vs] = y * (r * jax.nn.sigmoid(r))


def _gla(proj, wd_pad, bd, ng, B, S):
    C = GLA_CHUNK
    nc = S // C
    tri = jnp.asarray(np.tril(np.ones((C, C), np.float32)))
    qk_w = GLA_HEADS * GLA_DK
    v_w = GLA_HEADS * GLA_DV
    row = lambda b, c: b * nc + c
    return pl.pallas_call(
        _gla_kernel,
        out_shape=jax.ShapeDtypeStruct((B * S, v_w), f32),
        grid=(B, nc),
        in_specs=[
            pl.BlockSpec((C, qk_w), lambda b, c: (row(b, c), OFF_GQ // qk_w)),
            pl.BlockSpec((C, qk_w), lambda b, c: (row(b, c), OFF_GK // qk_w)),
            pl.BlockSpec((C, v_w), lambda b, c: (row(b, c), OFF_GV // v_w)),
            pl.BlockSpec((C, v_w), lambda b, c: (row(b, c), OFF_GR // v_w)),
            pl.BlockSpec((C, LANES), lambda b, c: (row(b, c), OFF_GLR // LANES)),
            pl.BlockSpec((LANES, qk_w), lambda b, c: (0, 0)),
            pl.BlockSpec((1, qk_w), lambda b, c: (0, 0)),
            pl.BlockSpec((1, GLA_DV), lambda b, c: (0, 0)),
            pl.BlockSpec((C, C), lambda b, c: (0, 0)),
        ],
        out_specs=pl.BlockSpec((C, v_w), lambda b, c: (row(b, c), 0)),
        scratch_shapes=[pltpu.VMEM((GLA_HEADS, GLA_DV, GLA_DK), f32)],
        compiler_params=_cp(("parallel", "arbitrary")),
        name="gla",
    )(proj, proj, proj, proj, proj, wd_pad, bd, ng, tri)


def _rope_kernel(q_ref, kc_ref, ks_ref, kw_ref, pos_ref, inv_ref, qo_ref, ko_ref):
    ang = pos_ref[...].astype(f32) * inv_ref[...]
    cos = jnp.cos(ang)
    sin = jnp.sin(ang)
    lane = lax.broadcasted_iota(i32, ang.shape, 1)
    half = ROPE_DIM // 2
    s_up = jnp.where((lane >= half) & (lane < ROPE_DIM), sin, 0.0)
    s_dn = jnp.where(lane < half, -sin, 0.0)

    def rot(x):
        return x * cos + pltpu.roll(x, half, 1) * s_up + pltpu.roll(x, LANES - half, 1) * s_dn

    for h in range(NSA_HEADS):
        sl = slice(h * NSA_DH, (h + 1) * NSA_DH)
        qo_ref[:, sl] = rot(q_ref[:, sl])
    kv_w = NSA_GROUPS * NSA_DH
    for n, ref in enumerate((kc_ref, ks_ref, kw_ref)):
        for g in range(NSA_GROUPS):
            sl = slice(g * NSA_DH, (g + 1) * NSA_DH)
            ko_ref[:, n * kv_w + g * NSA_DH:n * kv_w + (g + 1) * NSA_DH] = rot(ref[:, sl])


def _rope(proj, pos_col, inv_lane, tm):
    T = proj.shape[0]
    q_w = NSA_HEADS * NSA_DH
    kv_w = NSA_GROUPS * NSA_DH
    return pl.pallas_call(
        _rope_kernel,
        out_shape=(jax.ShapeDtypeStruct((T, q_w), f32), jax.ShapeDtypeStruct((T, 3 * kv_w), f32)),
        grid=(T // tm,),
        in_specs=[
            pl.BlockSpec((tm, q_w), lambda i: (i, OFF_NQ // q_w)),
            pl.BlockSpec((tm, kv_w), lambda i: (i, OFF_KCM // kv_w)),
            pl.BlockSpec((tm, kv_w), lambda i: (i, OFF_KSL // kv_w)),
            pl.BlockSpec((tm, kv_w), lambda i: (i, OFF_KWN // kv_w)),
            pl.BlockSpec((tm, 1), lambda i: (i, 0)),
            pl.BlockSpec((1, LANES), lambda i: (0, 0)),
        ],
        out_specs=(pl.BlockSpec((tm, q_w), lambda i: (i, 0)), pl.BlockSpec((tm, 3 * kv_w), lambda i: (i, 0))),
        compiler_params=_cp(("parallel",)),
        name="rope",
    )(proj, proj, proj, proj, pos_col, inv_lane)


def _compress_kernel(r_ref, pos_ref, w1_ref, w2_ref, o_ref):
    half = (CMP_BLOCK // 2) * NSA_DH
    r = r_ref[0]
    n_rows = r.shape[0]
    ra = (r + pos_ref[0:1, :]).astype(bf16)
    rb = (pltpu.roll(r, n_rows - 1, 0) + pos_ref[1:2, :]).astype(bf16)
    hid = _dot(ra, w1_ref[0:half, :]) + _dot(rb, w1_ref[half:2 * half, :])
    o_ref[0] = _dot(jax.nn.gelu(hid).astype(bf16), w2_ref[...])


def _compress(r, pos2, w1, w2):
    BG, n_rows, width = r.shape
    return pl.pallas_call(
        _compress_kernel,
        out_shape=jax.ShapeDtypeStruct((BG, n_rows, NSA_DH), f32),
        grid=(BG,),
        in_specs=[
            pl.BlockSpec((1, n_rows, width), lambda i: (i, 0, 0)),
            pl.BlockSpec((2, width), lambda i: (0, 0)),
            pl.BlockSpec((2 * width, NSA_DH), lambda i: (0, 0)),
            pl.BlockSpec((NSA_DH, NSA_DH), lambda i: (0, 0)),
        ],
        out_specs=pl.BlockSpec((1, n_rows, NSA_DH), lambda i: (i, 0, 0)),
        compiler_params=_cp(("parallel",)),
        name="compress",
    )(r, pos2, w1, w2)


def _cmp_sel_kernel(q_ref, kc_ref, vc_ref, at_ref, o_ref, sel_ref, *, tq, n_slc):
    i = pl.program_id(2)
    ncp = kc_ref.shape[1]
    scale = NSA_DH ** -0.5
    t = i * tq + lax.broadcasted_iota(i32, (tq, ncp), 0)
    n_end = lax.broadcasted_iota(i32, (tq, ncp), 1) * CMP_STRIDE + (CMP_BLOCK - 1)
    cmask = n_end <= t
    kc = kc_ref[0].astype(bf16)
    vc = vc_ref[0].astype(bf16)
    pg = jnp.zeros((tq, ncp), f32)
    for hp in range(NSA_HPG):
        sl = slice(hp * NSA_DH, (hp + 1) * NSA_DH)
        s = _nt(q_ref[:, sl].astype(bf16), kc) * scale
        s = jnp.where(cmask, s, NEG)
        m = jnp.max(s, axis=-1, keepdims=True)
        p = jnp.where(cmask, jnp.exp(s - m), 0.0)
        p = p / jnp.maximum(jnp.sum(p, axis=-1, keepdims=True), 1e-30)
        o_ref[:, sl] = _dot(p.astype(bf16), vc)
        pg = pg + p
    imp = _nt(at_ref[...], pg, HI)
    j = lax.broadcasted_iota(i32, (n_slc, tq), 0)
    blk = (i * tq + lax.broadcasted_iota(i32, (n_slc, tq), 1)) // SLC_BLOCK
    forced = (j == 0) | (j == blk) | (j == blk - 1)
    imp = jnp.where(forced, 1e9, jnp.where(j > blk, -1e9, imp))
    sel = jnp.zeros((n_slc, tq), f32)
    cur = imp
    for _ in range(min(SLC_TOPN, n_slc)):
        m = jnp.max(cur, axis=0, keepdims=True)
        first = jnp.min(jnp.where(cur == m, j, n_slc), axis=0, keepdims=True)
        pick = j == first
        sel = jnp.where(pick, 1.0, sel)
        cur = jnp.where(pick, -jnp.inf, cur)
    sel_ref[0] = sel.T


def _cmp_sel(q_rot, kc, vc, at, B, S, tq):
    nq = S // tq
    G = NSA_GROUPS
    ncp = kc.shape[1]
    n_slc = S // SLC_BLOCK
    gw = NSA_HPG * NSA_DH
    return pl.pallas_call(
        functools.partial(_cmp_sel_kernel, tq=tq, n_slc=n_slc),
        out_shape=(jax.ShapeDtypeStruct((B * S, NSA_HEADS * NSA_DH), f32),
                   jax.ShapeDtypeStruct((B * G, S, n_slc), f32)),
        grid=(B, G, nq),
        in_specs=[
            pl.BlockSpec((tq, gw), lambda b, g, i: (b * nq + i, g)),
            pl.BlockSpec((1, ncp, NSA_DH), lambda b, g, i: (b * G + g, 0, 0)),
            pl.BlockSpec((1, ncp, NSA_DH), lambda b, g, i: (b * G + g, 0, 0)),
            pl.BlockSpec((n_slc, ncp), lambda b, g, i: (0, 0)),
        ],
        out_specs=(pl.BlockSpec((tq, gw), lambda b, g, i: (b * nq + i, g)),
                   pl.BlockSpec((1, tq, n_slc), lambda b, g, i: (b * G + g, i, 0))),
        compiler_params=_cp(("parallel", "parallel", "parallel")),
        name="cmp_sel",
    )(q_rot, kc, vc, at)


def _slc_kernel(qi_ref, kj_ref, q_ref, k_ref, v_ref, sel_ref, o_ref, m_scr, l_scr, acc_scr, *, tq, tk, n_slc):
    p_id = pl.program_id(2)
    i = qi_ref[p_id]
    j = kj_ref[p_id]
    scale = NSA_DH ** -0.5

    @pl.when(j == 0)
    def _():
        m_scr[...] = jnp.full_like(m_scr, NEG)
        l_scr[...] = jnp.zeros_like(l_scr)
        acc_scr[...] = jnp.zeros_like(acc_scr)

    kblk = (j * tk + lax.broadcasted_iota(i32, (n_slc, tk), 1)) // SLC_BLOCK
    expand = jnp.where(kblk == lax.broadcasted_iota(i32, (n_slc, tk), 0), 1.0, 0.0).astype(bf16)
    picked = _dot(sel_ref[0].astype(bf16), expand)
    t = i * tq + lax.broadcasted_iota(i32, (tq, tk), 0)
    kpos = j * tk + lax.broadcasted_iota(i32, (tq, tk), 1)
    mask = jnp.where(kpos <= t, picked, 0.0) > 0.5
    k = k_ref[...].astype(bf16)
    v = v_ref[...].astype(bf16)
    for hp in range(NSA_HPG):
        sl = slice(hp * NSA_DH, (hp + 1) * NSA_DH)
        s = _nt(q_ref[:, sl].astype(bf16), k) * scale
        s = jnp.where(mask, s, NEG)
        m_old = m_scr[hp]
        m_new = jnp.maximum(m_old, jnp.max(s, axis=-1, keepdims=True))
        alpha = jnp.exp(m_old - m_new)
        p = jnp.exp(s - m_new)
        l_scr[hp] = alpha * l_scr[hp] + jnp.sum(p, axis=-1, keepdims=True)
        acc_scr[hp] = alpha * acc_scr[hp] + _dot(p.astype(bf16), v)
        m_scr[hp] = m_new

    @pl.when(j == ((i + 1) * tq - 1) // tk)
    def _():
        for hp in range(NSA_HPG):
            sl = slice(hp * NSA_DH, (hp + 1) * NSA_DH)
            o_ref[:, sl] = acc_scr[hp] / jnp.maximum(l_scr[hp], 1e-30)


def _slc(q_rot, k_rot, proj, sel_tm, B, S, tq, tk):
    nq, nk = S // tq, S // tk
    G = NSA_GROUPS
    n_slc = S // SLC_BLOCK
    gw = NSA_HPG * NSA_DH
    kv_w = NSA_GROUPS * NSA_DH
    pairs = [(i, j) for i in range(nq) for j in range(((i + 1) * tq - 1) // tk + 1)]
    qi = jnp.asarray([p[0] for p in pairs], i32)
    kj = jnp.asarray([p[1] for p in pairs], i32)
    grid_spec = pltpu.PrefetchScalarGridSpec(
        num_scalar_prefetch=2,
        grid=(B, G, len(pairs)),
        in_specs=[
            pl.BlockSpec((tq, gw), lambda b, g, p, qi, kj: (b * nq + qi[p], g)),
            pl.BlockSpec((tk, NSA_DH), lambda b, g, p, qi, kj: (b * nk + kj[p], kv_w // NSA_DH + g)),
            pl.BlockSpec((tk, NSA_DH), lambda b, g, p, qi, kj: (b * nk + kj[p], OFF_VSL // NSA_DH + g)),
            pl.BlockSpec((1, tq, n_slc), lambda b, g, p, qi, kj: (b * G + g, qi[p], 0)),
        ],
        out_specs=pl.BlockSpec((tq, gw), lambda b, g, p, qi, kj: (b * nq + qi[p], g)),
        scratch_shapes=[pltpu.VMEM((NSA_HPG, tq, 1), f32), pltpu.VMEM((NSA_HPG, tq, 1), f32),
                        pltpu.VMEM((NSA_HPG, tq, NSA_DH), f32)],
    )
    return pl.pallas_call(
        functools.partial(_slc_kernel, tq=tq, tk=tk, n_slc=n_slc),
        out_shape=jax.ShapeDtypeStruct((B * S, NSA_HEADS * NSA_DH), f32),
        grid_spec=grid_spec,
        compiler_params=_cp(("parallel", "parallel", "arbitrary")),
        name="slc",
    )(qi, kj, q_rot, k_rot, proj, sel_tm)


def _win_kernel(q_ref, k0_ref, k1_ref, k2_ref, v0_ref, v1_ref, v2_ref, o_ref, *, tq):
    i = pl.program_id(2)
    scale = NSA_DH ** -0.5
    nb = WINDOW // tq + 1
    k = jnp.concatenate([k0_ref[...], k1_ref[...], k2_ref[...]], axis=0).astype(bf16)
    v = jnp.concatenate([v0_ref[...], v1_ref[...], v2_ref[...]], axis=0).astype(bf16)
    t = i * tq + lax.broadcasted_iota(i32, (tq, nb * tq), 0)
    kpos = (i - (nb - 1)) * tq + lax.broadcasted_iota(i32, (tq, nb * tq), 1)
    mask = jnp.where(kpos >= 0, jnp.where(kpos <= t, jnp.where(kpos > t - WINDOW, 1.0, 0.0), 0.0), 0.0) > 0.5
    for hp in range(NSA_HPG):
        sl = slice(hp * NSA_DH, (hp + 1) * NSA_DH)
        s = _nt(q_ref[:, sl].astype(bf16), k) * scale
        s = jnp.where(mask, s, NEG)
        m = jnp.max(s, axis=-1, keepdims=True)
        p = jnp.where(mask, jnp.exp(s - m), 0.0)
        p = p / jnp.maximum(jnp.sum(p, axis=-1, keepdims=True), 1e-30)
        o_ref[:, sl] = _dot(p.astype(bf16), v)


def _win(q_rot, k_rot, proj, B, S, tq):
    assert WINDOW % tq == 0 and WINDOW // tq == 2
    nq = S // tq
    gw = NSA_HPG * NSA_DH
    kv_w = NSA_GROUPS * NSA_DH

    def kspec(d, col0):
        return pl.BlockSpec((tq, NSA_DH), lambda b, g, i: (b * nq + jnp.maximum(i - 2 + d, 0), col0 + g))

    return pl.pallas_call(
        functools.partial(_win_kernel, tq=tq),
        out_shape=jax.ShapeDtypeStruct((B * S, NSA_HEADS * NSA_DH), f32),
        grid=(B, NSA_GROUPS, nq),
        in_specs=[pl.BlockSpec((tq, gw), lambda b, g, i: (b * nq + i, g))]
        + [kspec(d, 2 * kv_w // NSA_DH) for d in range(3)]
        + [kspec(d, OFF_VWN // NSA_DH) for d in range(3)],
        out_specs=pl.BlockSpec((tq, gw), lambda b, g, i: (b * nq + i, g)),
        compiler_params=_cp(("parallel", "parallel", "parallel")),
        name="win",
    )(q_rot, k_rot, k_rot, k_rot, proj, proj, proj)


def _merge_kernel(yg_ref, oc_ref, os_ref, ow_ref, ng_ref, ga_ref, gb_ref, wg_ref, wn_ref, o_ref, yn_scr):
    sg = jax.nn.sigmoid(ng_ref[...])
    for h in range(NSA_HEADS):
        sl = slice(h * NSA_DH, (h + 1) * NSA_DH)
        y = (sg[:, 3 * h:3 * h + 1] * oc_ref[:, sl] + sg[:, 3 * h + 1:3 * h + 2] * os_ref[:, sl]
             + sg[:, 3 * h + 2:3 * h + 3] * ow_ref[:, sl])
        yn_scr[:, sl] = y.astype(bf16)
    up_g = _dot(yg_ref[...].astype(bf16), wg_ref[...])
    up_n = _dot(yn_scr[...], wn_ref[...])
    o_ref[...] = (jax.nn.sigmoid(ga_ref[...]) * up_g + jax.nn.sigmoid(gb_ref[...]) * up_n).astype(o_ref.dtype)


def _merge(y_gla, o_cmp, o_slc, o_win, proj, wg, wn, tm):
    T = y_gla.shape[0]
    qw = NSA_HEADS * NSA_DH
    gvw = GLA_HEADS * GLA_DV
    const = dict(pipeline_mode=pl.Buffered(1))
    return pl.pallas_call(
        _merge_kernel,
        out_shape=jax.ShapeDtypeStruct((T, D_MODEL), bf16),
        grid=(T // tm,),
        in_specs=[
            pl.BlockSpec((tm, gvw), lambda i: (i, 0)),
            pl.BlockSpec((tm, qw), lambda i: (i, 0)),
            pl.BlockSpec((tm, qw), lambda i: (i, 0)),
            pl.BlockSpec((tm, qw), lambda i: (i, 0)),
            pl.BlockSpec((tm, LANES), lambda i: (i, OFF_NGATE // LANES)),
            pl.BlockSpec((tm, D_MODEL), lambda i: (i, OFF_GA // D_MODEL)),
            pl.BlockSpec((tm, D_MODEL), lambda i: (i, OFF_GB // D_MODEL)),
            pl.BlockSpec((gvw, D_MODEL), lambda i: (0, 0), **const),
            pl.BlockSpec((qw, D_MODEL), lambda i: (0, 0), **const),
        ],
        out_specs=pl.BlockSpec((tm, D_MODEL), lambda i: (i, 0)),
        scratch_shapes=[pltpu.VMEM((tm, qw), bf16)],
        compiler_params=_cp(("parallel",)),
        name="merge",
    )(y_gla, o_cmp, o_slc, o_win, proj, proj, proj, wg, wn)


def _outp_kernel(m_ref, x_ref, wo_ref, g2_ref, wq_ref, x1_ref, h2_ref, qp_ref):
    x1 = x_ref[...] + _dot(m_ref[...], wo_ref[...])
    x1_ref[...] = x1
    ms = jnp.mean(x1 * x1, axis=-1, keepdims=True)
    h2 = x1 * lax.rsqrt(ms + EPS) * g2_ref[...]
    h2_ref[...] = h2
    qp_ref[...] = _dot(h2.astype(bf16), wq_ref[...])


def _outp(merged, x2, wo, g2, wq, tm):
    T = x2.shape[0]
    pw = PEER_HEADS * PEER_DKEY
    const = dict(pipeline_mode=pl.Buffered(1))
    return pl.pallas_call(
        _outp_kernel,
        out_shape=(jax.ShapeDtypeStruct((T, D_MODEL), f32), jax.ShapeDtypeStruct((T, D_MODEL), f32),
                   jax.ShapeDtypeStruct((T, pw), f32)),
        grid=(T // tm,),
        in_specs=[
            pl.BlockSpec((tm, D_MODEL), lambda i: (i, 0)),
            pl.BlockSpec((tm, D_MODEL), lambda i: (i, 0)),
            pl.BlockSpec((D_MODEL, D_MODEL), lambda i: (0, 0), **const),
            pl.BlockSpec((1, D_MODEL), lambda i: (0, 0)),
            pl.BlockSpec((D_MODEL, pw), lambda i: (0, 0), **const),
        ],
        out_specs=(pl.BlockSpec((tm, D_MODEL), lambda i: (i, 0)), pl.BlockSpec((tm, D_MODEL), lambda i: (i, 0)),
                   pl.BlockSpec((tm, pw), lambda i: (i, 0))),
        compiler_params=_cp(("parallel",)),
        name="outp",
    )(merged, x2, wo, g2, wq)


def _topk_rows(cur, iota0, n, k, payload=None):
    vals, outs = [], []
    for _ in range(k):
        m = jnp.max(cur, axis=0, keepdims=True)
        first = jnp.min(jnp.where(cur == m, iota0, n), axis=0, keepdims=True)
        pick = iota0 == first
        vals.append(m)
        outs.append(first if payload is None else jnp.sum(jnp.where(pick, payload, 0), axis=0, keepdims=True))
        cur = jnp.where(pick, -jnp.inf, cur)
    return jnp.concatenate(vals, axis=0), jnp.concatenate(outs, axis=0)


_PEER_CELLS = [(a, b) for a in range(PEER_TOPK) for b in range(PEER_TOPK // (a + 1))]
_PEER_NCELL = -(-len(_PEER_CELLS) // 8) * 8


def _ptopk_kernel(qp_ref, k1_ref, k2_ref, e_ref, g_ref, *, tt):
    K = PEER_TOPK
    half = PEER_DKEY // 2
    io_keys = lax.broadcasted_iota(i32, (PEER_NKEYS, tt), 0)
    io_cand = lax.broadcasted_iota(i32, (_PEER_NCELL, tt), 0)
    starts = [r for r, (a, b) in enumerate(_PEER_CELLS) if b == 0]
    a_row = jnp.zeros((_PEER_NCELL, tt), i32)
    start_row = jnp.zeros((_PEER_NCELL, tt), i32)
    for st in starts[1:]:
        a_row = a_row + jnp.where(io_cand >= st, 1, 0)
        start_row = jnp.where(io_cand >= st, st, start_row)
    b_row = io_cand - start_row
    real = io_cand < len(_PEER_CELLS)
    gate_rows, expert_rows = [], []
    for h in range(PEER_HEADS):
        q1 = qp_ref[:, h * PEER_DKEY:h * PEER_DKEY + half]
        q2 = qp_ref[:, h * PEER_DKEY + half:(h + 1) * PEER_DKEY]
        s1 = _nt(k1_ref[h], q1, HI)
        s2 = _nt(k2_ref[h], q2, HI)
        v1, i1 = _topk_rows(s1, io_keys, PEER_NKEYS, K)
        v2, i2 = _topk_rows(s2, io_keys, PEER_NKEYS, K)
        va = jnp.zeros((_PEER_NCELL, tt), f32)
        vb = jnp.zeros((_PEER_NCELL, tt), f32)
        ea = jnp.zeros((_PEER_NCELL, tt), i32)
        eb = jnp.zeros((_PEER_NCELL, tt), i32)
        for r in range(K):
            va = jnp.where(a_row == r, v1[r:r + 1], va)
            ea = jnp.where(a_row == r, i1[r:r + 1], ea)
            vb = jnp.where(b_row == r, v2[r:r + 1], vb)
            eb = jnp.where(b_row == r, i2[r:r + 1], eb)
        cand = jnp.where(real, va + vb, -jnp.inf)
        ecand = ea * PEER_NKEYS + eb
        vals, experts = _topk_rows(cand, io_cand, _PEER_NCELL, K, payload=ecand)
        p = jnp.exp(vals - vals[0:1])
        gate_rows.append(p / jnp.sum(p, axis=0, keepdims=True))
        expert_rows.append(experts)
    g_ref[...] = jnp.concatenate(gate_rows, axis=0).T
    e_bits = pltpu.bitcast(jnp.concatenate(expert_rows, axis=0), f32)
    e_ref[...] = pltpu.bitcast(e_bits.T, i32)


def _ptopk(qp, k1, k2, tt):
    T = qp.shape[0]
    pw = PEER_HEADS * PEER_DKEY
    return pl.pallas_call(
        functools.partial(_ptopk_kernel, tt=tt),
        out_shape=(jax.ShapeDtypeStruct((T, PEER_SEL), i32), jax.ShapeDtypeStruct((T, PEER_SEL), f32)),
        grid=(T // tt,),
        in_specs=[
            pl.BlockSpec((tt, pw), lambda i: (i, 0)),
            pl.BlockSpec((PEER_HEADS, PEER_NKEYS, PEER_DKEY // 2), lambda i: (0, 0, 0)),
            pl.BlockSpec((PEER_HEADS, PEER_NKEYS, PEER_DKEY // 2), lambda i: (0, 0, 0)),
        ],
        out_specs=(pl.BlockSpec((tt, PEER_SEL), lambda i: (i, 0)), pl.BlockSpec((tt, PEER_SEL), lambda i: (i, 0))),
        compiler_params=_cp(("parallel",)),
        name="ptopk",
    )(qp, k1, k2)


PEER_ROWS = D_MODEL // LANES
PEER_SEL = PEER_HEADS * PEER_TOPK


def _pack_expert_table(peer_u, peer_v):
    ub = lax.bitcast_convert_type(peer_u.astype(bf16), jnp.uint16).astype(jnp.uint32)
    vb = lax.bitcast_convert_type(peer_v.astype(bf16), jnp.uint16).astype(jnp.uint32)
    return (ub | (vb << 16)).reshape(-1, PEER_ROWS, LANES)


def _pexp_kernel(idx_ref, idx_next_ref, g_ref, x_ref, x1_ref, gf_ref, tab_ref, o_ref, a0, a1, b0, b1, sem, *, tt):
    i = pl.program_id(0)
    n_steps = pl.num_programs(0)
    th = tt // 2

    def start(src_idx_ref, t, k, halves, sems):
        e = src_idx_ref[0, 0, t * PEER_SEL + k]
        h, tl = divmod(t, th)
        pltpu.make_async_copy(tab_ref.at[e], halves[h].at[:, tl * PEER_SEL + k, :], sems[h]).start(priority=k % 2)

    def wait_all(dst, dst_sem):
        pltpu.make_async_copy(dst, dst, dst_sem).wait()

    def compute(src, tl, t):
        rows = slice(tl * PEER_SEL, (tl + 1) * PEER_SEL)
        acc = jnp.zeros((PEER_SEL, LANES), f32)
        for s in range(PEER_ROWS):
            u = pltpu.bitcast(src[s, rows, :] << 16, f32)
            acc = acc + u * x_ref[0, t:t + 1, s * LANES:(s + 1) * LANES]
        hid = jnp.sum(acc.T, axis=0, keepdims=True)
        w = (g_ref[0, t:t + 1, :] * jax.nn.gelu(hid)).astype(bf16)
        for s in range(PEER_ROWS):
            v = pltpu.bitcast(src[s, rows, :] & jnp.uint32(0xFFFF0000), f32).astype(bf16)
            o_ref[0, t:t + 1, s * LANES:(s + 1) * LANES] = _dot(w, v)

    def step(cur, cur_sems, nxt, nxt_sems):
        for h in range(2):
            wait_all(cur[h], cur_sems[h])
            for tl in range(th):
                t = h * th + tl
                for k in range(PEER_SEL):
                    start(idx_next_ref, t, k, nxt, nxt_sems)
                compute(cur[h], tl, t)
        y = x1_ref[0] + o_ref[0]
        ms = jnp.mean(y * y, axis=-1, keepdims=True)
        o_ref[0] = y * lax.rsqrt(ms + EPS) * gf_ref[...]

    bufs_a, sems_a = (a0, a1), (sem.at[0], sem.at[1])
    bufs_b, sems_b = (b0, b1), (sem.at[2], sem.at[3])

    @pl.when(i == 0)
    def _():
        for h in range(2):
            def body(n, carry, h=h):
                e = idx_ref[0, 0, h * th * PEER_SEL + n]
                pltpu.make_async_copy(tab_ref.at[e], bufs_a[h].at[:, n, :], sems_a[h]).start()
                return carry
            lax.fori_loop(0, th * PEER_SEL, body, 0, unroll=8)

    @pl.when(lax.rem(i, 2) == 0)
    def _():
        step(bufs_a, sems_a, bufs_b, sems_b)

    @pl.when(lax.rem(i, 2) == 1)
    def _():
        step(bufs_b, sems_b, bufs_a, sems_a)

    @pl.when((i == n_steps - 1) & (lax.rem(i, 2) == 0))
    def _():
        wait_all(b0, sems_b[0])
        wait_all(b1, sems_b[1])

    @pl.when((i == n_steps - 1) & (lax.rem(i, 2) == 1))
    def _():
        wait_all(a0, sems_a[0])
        wait_all(a1, sems_a[1])


def _pexp(idx3, gates, h2_3d, x1_3d, gf, table, tt):
    n_steps = gates.shape[0]
    rows = pl.BlockSpec((1, tt, D_MODEL), lambda i: (i, 0, 0))
    return pl.pallas_call(
        functools.partial(_pexp_kernel, tt=tt),
        out_shape=jax.ShapeDtypeStruct((n_steps, tt, D_MODEL), f32),
        grid=(n_steps,),
        in_specs=[
            pl.BlockSpec((1, 1, tt * PEER_SEL), lambda i: (i, 0, 0), memory_space=pltpu.SMEM),
            pl.BlockSpec((1, 1, tt * PEER_SEL), lambda i: (jnp.minimum(i + 1, n_steps - 1), 0, 0),
                         memory_space=pltpu.SMEM),
            pl.BlockSpec((1, tt, PEER_SEL), lambda i: (i, 0, 0)),
            rows,
            rows,
            pl.BlockSpec((1, D_MODEL), lambda i: (0, 0)),
            pl.BlockSpec(memory_space=pl.ANY),
        ],
        out_specs=rows,
        scratch_shapes=[pltpu.VMEM((PEER_ROWS, (tt // 2) * PEER_SEL, LANES), jnp.uint32) for _ in range(4)]
        + [pltpu.SemaphoreType.DMA((4,))],
        compiler_params=_cp(("arbitrary",)),
        name="pexp",
    )(idx3, idx3, gates, h2_3d, x1_3d, gf, table)


def _pack_w_in(w):
    zpad = lambda a, n: jnp.pad(a, ((0, 0), (0, n - a.shape[1])))
    return jnp.concatenate(
        [w[:, 0:5120], w[:, 5168:7216], w[:, 7232:12352], zpad(w[:, 5120:5168], LANES), zpad(w[:, 7216:7232], LANES)],
        axis=1).astype(bf16)


def _importance_map(n_cmp_pad, n_slc):
    ratio = SLC_BLOCK // CMP_STRIDE
    nrep = CMP_BLOCK // CMP_STRIDE
    at = np.zeros((n_slc, n_cmp_pad), np.float32)
    for j in range(n_slc):
        for m in range(ratio):
            for n in range(nrep):
                c = ratio * j + m - n
                if 0 <= c < n_cmp_pad:
                    at[j, c] += 1.0
    return jnp.asarray(at)


def _mixers(x2, positions, norm1_g, w_in, gla_w_decay, gla_b_decay, gla_norm_g, pos_k, pos_v, k_w1, k_w2, v_w1, v_w2,
            B, S):
    T = B * S
    G = NSA_GROUPS
    proj = _proj(x2, norm1_g.reshape(1, D_MODEL), _pack_w_in(w_in), tm=min(1024, T), tn=896)

    wd_pad = jnp.pad(gla_w_decay, ((0, LANES - GLA_LOWRANK), (0, 0)))
    y_gla = _gla(proj, wd_pad, gla_b_decay.reshape(1, -1), gla_norm_g.reshape(1, GLA_DV), B, S)

    inv = jnp.power(ROPE_THETA, -(jnp.arange(0, ROPE_DIM, 2, dtype=f32) / ROPE_DIM))
    inv_lane = jnp.concatenate([inv, inv, jnp.zeros((LANES - ROPE_DIM,), f32)]).reshape(1, LANES)
    q_rot, k_rot = _rope(proj, positions.reshape(T, 1), inv_lane, tm=min(512, T))

    n_rows = S // CMP_STRIDE
    kv_w = G * NSA_DH

    def stride_rows(a):
        return a.reshape(B, S, G, NSA_DH).transpose(0, 2, 1, 3).reshape(B * G, n_rows, CMP_STRIDE * NSA_DH)

    r_k = stride_rows(k_rot[:, 0:kv_w])
    r_v = stride_rows(proj[:, OFF_VCM:OFF_VCM + kv_w])
    pos2 = lambda p: p.reshape(2, (CMP_BLOCK // 2) * NSA_DH)
    kc = _compress(r_k, pos2(pos_k), k_w1.astype(bf16), k_w2.astype(bf16))
    vc = _compress(r_v, pos2(pos_v), v_w1.astype(bf16), v_w2.astype(bf16))

    n_slc = S // SLC_BLOCK
    o_cmp, sel_tm = _cmp_sel(q_rot, kc, vc, _importance_map(n_rows, n_slc), B, S, tq=min(256, S))
    o_slc = _slc(q_rot, k_rot, proj, sel_tm, B, S, tq=256, tk=min(512, S))
    o_win = _win(q_rot, k_rot, proj, B, S, tq=256)
    return proj, y_gla, o_cmp, o_slc, o_win


def _peer_stage(x1, h2, qp, sub_k1, sub_k2, peer_u, peer_v, final_g):
    T = x1.shape[0]
    experts, gate_vals = _ptopk(qp, sub_k1, sub_k2, tt=256)
    tt = 16
    idx3 = experts.reshape(T // tt, 1, tt * PEER_SEL)
    gates = gate_vals.reshape(T // tt, tt, PEER_SEL)
    table = _pack_expert_table(peer_u, peer_v)
    blocks = lambda a: a.reshape(T // tt, tt, D_MODEL)
    out = _pexp(idx3, gates, blocks(h2), blocks(x1), final_g.reshape(1, D_MODEL), table, tt)
    return out.reshape(T, D_MODEL)


def kernel(x, positions, norm1_g, w_in, gla_w_decay, gla_b_decay, gla_norm_g, nsa_cmp_pos_k, nsa_cmp_pos_v,
           nsa_cmp_k_w1, nsa_cmp_k_w2, nsa_cmp_v_w1, nsa_cmp_v_w2, w_up_gla, w_up_nsa, w_out, norm2_g,
           peer_w_q, peer_sub_k1, peer_sub_k2, peer_u, peer_v, final_g):
    B, S, D = x.shape
    T = B * S
    assert norm1_g.shape[0] == 1 and D == D_MODEL
    x2 = x.reshape(T, D)
    proj, y_gla, o_cmp, o_slc, o_win = _mixers(
        x2, positions, norm1_g[0], w_in[0], gla_w_decay[0], gla_b_decay[0], gla_norm_g[0], nsa_cmp_pos_k[0],
        nsa_cmp_pos_v[0], nsa_cmp_k_w1[0], nsa_cmp_k_w2[0], nsa_cmp_v_w1[0], nsa_cmp_v_w2[0], B, S)
    merged = _merge(y_gla, o_cmp, o_slc, o_win, proj, w_up_gla[0].astype(bf16), w_up_nsa[0].astype(bf16), tm=256)
    x1, h2, qp = _outp(merged, x2, w_out[0].astype(bf16), norm2_g[0].reshape(1, D), peer_w_q[0].astype(bf16), tm=256)
    out = _peer_stage(x1, h2, qp, peer_sub_k1[0], peer_sub_k2[0], peer_u[0], peer_v[0], final_g)
    return out.reshape(B, S, D)
```
